```python
import jax, jax.numpy as jnp
from jax import lax
import numpy as np

D_MODEL = 1024
BATCH = 8
SEQ = 2048
DEPTH = 2

GRID_W = 64
CTX_LEN = 256
HEAD_DIM = 64
ATTN_BLOCK = 128
WINDOW = 128
ROPE_THETA = 10000.0
NORM_EPS = 1e-6
N_MOD = 9
N_BRANCHES = 3
D_FF = 2816
A_HEADS = 8
A_KV_HEADS = 2
B_HEADS = 8
B_KV_HEADS = 2
C_HEADS = 8
C_NOPE_DIM = 64
C_ROPE_DIM = 32
C_QK_DIM = C_NOPE_DIM + C_ROPE_DIM
C_V_DIM = 64
C_Q_RANK = 384
C_KV_RANK = 256

A_Q = A_HEADS * HEAD_DIM
A_KV = A_KV_HEADS * HEAD_DIM
B_Q = B_HEADS * HEAD_DIM
B_KV = B_KV_HEADS * HEAD_DIM
C_OUT = C_HEADS * C_V_DIM
KV_SPLITS = (A_KV, A_KV, B_KV, B_KV, C_KV_RANK, C_ROPE_DIM)
Q_SPLITS = (A_Q, B_Q, C_Q_RANK)
GATE_COLS = N_BRANCHES * D_MODEL
KV_COLS = sum(KV_SPLITS)
IN_COLS = KV_COLS + sum(Q_SPLITS) + GATE_COLS

kernel_name = "hybrid_gated_tri_attention_dit_block"


def rms_norm(x, g):
    xf = x.astype(jnp.float32)
    y = xf * lax.rsqrt(jnp.mean(jnp.square(xf), axis=-1, keepdims=True) + NORM_EPS)
    return (y * g.astype(jnp.float32)).astype(x.dtype)


def adaln(h, g, shift, scale):
    return rms_norm(h, g) * (1 + scale) + shift


def swiglu(h, w_in, w_out):
    gate, up = jnp.split(h @ w_in, 2, axis=-1)
    return (jax.nn.silu(gate) * up) @ w_out


def split_cols(t, sizes):
    idx = [int(i) for i in np.cumsum(sizes)[:-1]]
    return jnp.split(t, idx, axis=-1)


def heads(t, n_heads):
    return t.reshape(t.shape[0], t.shape[1], n_heads, -1)


def axial_rope_tables(n_tokens, rot_dim):
    rows = n_tokens // GRID_W
    row = jnp.repeat(jnp.arange(rows, dtype=jnp.float32), GRID_W)
    col = jnp.tile(jnp.arange(GRID_W, dtype=jnp.float32), rows)
    n_freq = rot_dim // 4
    inv_freq = ROPE_THETA ** (-jnp.arange(n_freq, dtype=jnp.float32) / n_freq)
    ang = jnp.concatenate([row[:, None] * inv_freq, col[:, None] * inv_freq], axis=-1)
    return jnp.cos(ang)[:, None, :], jnp.sin(ang)[:, None, :]


def apply_rope(x, rope):
    cos, sin = rope
    cos = cos.astype(x.dtype)
    sin = sin.astype(x.dtype)
    x1, x2 = x[..., 0::2], x[..., 1::2]
    return jnp.stack([x1 * cos - x2 * sin, x1 * sin + x2 * cos], axis=-1).reshape(x.shape)


def qk_prep(t, n_heads, g, rope):
    t = rms_norm(heads(t, n_heads), g)
    return t if rope is None else apply_rope(t, rope)


def rope_tail(t, n_rot, rope):
    if rope is None:
        return t
    return jnp.concatenate([t[..., :-n_rot], apply_rope(t[..., -n_rot:], rope)], axis=-1)


def mla_keys_values(c_kv, k_rope, p, rope):
    kv = heads(rms_norm(c_kv, p["c_kv_lat_norm"]) @ p["c_w_ukv"], C_HEADS)
    k_nope, v = kv[..., :C_NOPE_DIM], kv[..., C_NOPE_DIM:]
    k_rope = jnp.broadcast_to(k_rope[:, :, None, :], k_nope.shape[:-1] + (C_ROPE_DIM,))
    k = rms_norm(jnp.concatenate([k_nope, k_rope], axis=-1), p["c_k_norm"])
    return rope_tail(k, C_ROPE_DIM, rope), v


def mla_queries(c_q, p, rope):
    q = heads(rms_norm(c_q, p["c_q_lat_norm"]) @ p["c_w_uq"], C_HEADS)
    return rope_tail(rms_norm(q, p["c_q_norm"]), C_ROPE_DIM, rope)


def blocked_gqa(q, k, v, sink=None):
    bsz, n_q, n_heads, dqk = q.shape
    n_kv = k.shape[2]
    grp = n_heads // n_kv
    dv = v.shape[-1]
    nb = n_q // ATTN_BLOCK
    scale = dqk ** -0.5
    qb = jnp.moveaxis(q.reshape(bsz, nb, ATTN_BLOCK, n_kv, grp, dqk), 1, 0)

    def attend(q_blk):
        s = jnp.einsum('bqkgd,bmkd->bkgqm', q_blk, k).astype(jnp.float32) * scale
        if sink is not None:
            s_sink = jnp.broadcast_to(sink.astype(jnp.float32).reshape(n_kv, grp, 1, 1), s.shape[:-1] + (1,))
            s = jnp.concatenate([s, s_sink], axis=-1)
        p = jax.nn.softmax(s, axis=-1).astype(v.dtype)
        if sink is not None:
            p = p[..., :-1]
        return jnp.einsum('bkgqm,bmkd->bqkgd', p, v)

    o = lax.map(attend, qb)
    return jnp.moveaxis(o, 0, 1).reshape(bsz, n_q, n_heads * dv)


def windowed_gqa_with_sink(q, k, v, k_ctx, v_ctx, sink):
    bsz, seq, n_heads, dh = q.shape
    n_kv = k.shape[2]
    grp = n_heads // n_kv
    n_ctx = k_ctx.shape[1]
    nb = seq // ATTN_BLOCK
    halo = (WINDOW // ATTN_BLOCK) * ATTN_BLOCK
    span = ATTN_BLOCK + 2 * halo
    scale = dh ** -0.5
    pad = ((0, 0), (halo, halo), (0, 0), (0, 0))
    kp = jnp.pad(k, pad)
    vp = jnp.pad(v, pad)
    qb = jnp.moveaxis(q.reshape(bsz, nb, ATTN_BLOCK, n_kv, grp, dh), 1, 0)
    offs = jnp.arange(span) - halo
    rel = offs[None, :] - jnp.arange(ATTN_BLOCK)[:, None]
    sink_f = sink.astype(jnp.float32).reshape(n_kv, grp, 1, 1)

    def attend(args):
        n, q_blk = args
        start = n * ATTN_BLOCK
        k_blk = lax.dynamic_slice_in_dim(kp, start, span, axis=1)
        v_blk = lax.dynamic_slice_in_dim(vp, start, span, axis=1)
        k_pos = start + offs
        valid = (jnp.abs(rel) <= WINDOW) & ((k_pos >= 0) & (k_pos < seq))[None, :]
        s_loc = jnp.einsum('bqkgd,bmkd->bkgqm', q_blk, k_blk).astype(jnp.float32) * scale
        s_loc = jnp.where(valid, s_loc, -jnp.inf)
        s_ctx = jnp.einsum('bqkgd,bmkd->bkgqm', q_blk, k_ctx).astype(jnp.float32) * scale
        s_sink = jnp.broadcast_to(sink_f, s_ctx.shape[:-1] + (1,))
        p = jax.nn.softmax(jnp.concatenate([s_loc, s_ctx, s_sink], axis=-1), axis=-1).astype(v.dtype)
        return (jnp.einsum('bkgqm,bmkd->bqkgd', p[..., :span], v_blk)
                + jnp.einsum('bkgqm,bmkd->bqkgd', p[..., span:span + n_ctx], v_ctx))

    o = lax.map(attend, (jnp.arange(nb), qb))
    return jnp.moveaxis(o, 0, 1).reshape(bsz, seq, n_heads * dh)


def merge_branches(ya, yb, yc, gates, p):
    ga, gb, gc = jnp.split(gates, N_BRANCHES, axis=-1)
    m = (jax.nn.sigmoid(ga) * (ya @ p["w_branch_a"])
         + jax.nn.sigmoid(gb) * (yb @ p["w_branch_b"])
         + jax.nn.sigmoid(gc) * (yc @ p["w_branch_c"]))
    return m @ p["w_out"]


def token_mixers(ux, uz, p, rope_hd, rope_c, ctx_out):
    w_in = p["w_in"]
    ak, av, bk, bv, ckv, ckr = split_cols(uz @ w_in[:, :KV_COLS], KV_SPLITS)
    ka_z = qk_prep(ak, A_KV_HEADS, p["a_k_norm"], None)
    va_z = heads(av, A_KV_HEADS)
    kb_z = qk_prep(bk, B_KV_HEADS, p["b_k_norm"], None)
    vb_z = heads(bv, B_KV_HEADS)
    kc_z, vc_z = mla_keys_values(ckv, ckr, p, None)
    ak, av, bk, bv, ckv, ckr, aq, bq, cq, gates = split_cols(ux @ w_in, KV_SPLITS + Q_SPLITS + (GATE_COLS,))
    ka_x = qk_prep(ak, A_KV_HEADS, p["a_k_norm"], rope_hd)
    va_x = heads(av, A_KV_HEADS)
    kb_x = qk_prep(bk, B_KV_HEADS, p["b_k_norm"], rope_hd)
    vb_x = heads(bv, B_KV_HEADS)
    kc_x, vc_x = mla_keys_values(ckv, ckr, p, rope_c)
    qa_x = qk_prep(aq, A_HEADS, p["a_q_norm"], rope_hd)
    qb_x = qk_prep(bq, B_HEADS, p["b_q_norm"], rope_hd)
    qc_x = mla_queries(cq, p, rope_c)
    ya = windowed_gqa_with_sink(qa_x, ka_x, va_x, ka_z, va_z, p["a_sink"])
    yb = blocked_gqa(qb_x, jnp.concatenate([kb_z, kb_x], axis=1), jnp.concatenate([vb_z, vb_x], axis=1))
    yc = blocked_gqa(qc_x, jnp.concatenate([kc_z, kc_x], axis=1), jnp.concatenate([vc_z, vc_x], axis=1))
    mix_x = merge_branches(ya, yb, yc, gates, p)
    if not ctx_out:
        return mix_x, None
    aq, bq, cq, gates = split_cols(uz @ w_in[:, KV_COLS:], Q_SPLITS + (GATE_COLS,))
    ya = blocked_gqa(qk_prep(aq, A_HEADS, p["a_q_norm"], None), ka_z, va_z, sink=p["a_sink"])
    yb = blocked_gqa(qk_prep(bq, B_HEADS, p["b_q_norm"], None), kb_z, vb_z)
    yc = blocked_gqa(mla_queries(cq, p, None), kc_z, vc_z)
    return mix_x, merge_branches(ya, yb, yc, gates, p)


def setup_inputs(seed: int = 0) -> dict:
    key = jax.random.key(seed)
    ks = jax.random.split(key, 32)
    f32 = jnp.float32

    def w(k, shape, fan_in, mult=1.0):
        return jax.random.normal(k, shape, f32) * (mult * fan_in ** -0.5)

    def gain(k, shape):
        return 1.0 + 0.02 * jax.random.normal(k, shape, f32)

    D = D_MODEL
    return {
        "x": jax.random.normal(ks[0], (BATCH, SEQ, D), f32),
        "c": jax.random.normal(ks[1], (BATCH, D), f32),
        "ctx": jax.random.normal(ks[2], (BATCH, CTX_LEN, D), f32),
        "c_ctx": jax.random.normal(ks[3], (D,), f32),
        "ada_w": w(ks[4], (DEPTH, D, N_MOD * D), D, 0.5),
        "ada_b": 0.02 * jax.random.normal(ks[5], (DEPTH, N_MOD * D), f32),
        "ffn1_norm": gain(ks[6], (DEPTH, D)),
        "ffn1_w_in": w(ks[7], (DEPTH, D, 2 * D_FF), D),
        "ffn1_w_out": w(ks[8], (DEPTH, D_FF, D), D_FF),
        "mix_norm": gain(ks[9], (DEPTH, D)),
        "mix_w_in": w(ks[10], (DEPTH, D, IN_COLS), D),
        "a_q_norm": gain(ks[11], (DEPTH, HEAD_DIM)),
        "a_k_norm": gain(ks[12], (DEPTH, HEAD_DIM)),
        "a_sink": 0.5 * jax.random.normal(ks[13], (DEPTH, A_HEADS), f32),
        "b_q_norm": gain(ks[14], (DEPTH, HEAD_DIM)),
        "b_k_norm": gain(ks[15], (DEPTH, HEAD_DIM)),
        "c_q_lat_norm": gain(ks[16], (DEPTH, C_Q_RANK)),
        "c_w_uq": w(ks[17], (DEPTH, C_Q_RANK, C_HEADS * C_QK_DIM), C_Q_RANK),
        "c_kv_lat_norm": gain(ks[18], (DEPTH, C_KV_RANK)),
        "c_w_ukv": w(ks[19], (DEPTH, C_KV_RANK, C_HEADS * (C_NOPE_DIM + C_V_DIM)), C_KV_RANK),
        "c_q_norm": gain(ks[20], (DEPTH, C_QK_DIM)),
        "c_k_norm": gain(ks[21], (DEPTH, C_QK_DIM)),
        "w_branch_a": w(ks[22], (DEPTH, A_Q, D), A_Q),
        "w_branch_b": w(ks[23], (DEPTH, B_Q, D), B_Q),
        "w_branch_c": w(ks[24], (DEPTH, C_OUT, D), C_OUT),
        "mix_w_out": w(ks[25], (DEPTH, D, D), D),
        "ffn2_norm": gain(ks[26], (DEPTH, D)),
        "ffn2_w_in": w(ks[27], (DEPTH, D, 2 * D_FF), D),
        "ffn2_w_out": w(ks[28], (DEPTH, D_FF, D), D_FF),
    }


def reference(x, c, ctx, c_ctx, ada_w, ada_b, ffn1_norm, ffn1_w_in, ffn1_w_out, mix_norm, mix_w_in,
              a_q_norm, a_k_norm, a_sink, b_q_norm, b_k_norm, c_q_lat_norm, c_w_uq, c_kv_lat_norm, c_w_ukv,
              c_q_norm, c_k_norm, w_branch_a, w_branch_b, w_branch_c, mix_w_out, ffn2_norm, ffn2_w_in,
              ffn2_w_out):
    n_lat = x.shape[1]
    rope_hd = axial_rope_tables(n_lat, HEAD_DIM)
    rope_c = axial_rope_tables(n_lat, C_ROPE_DIM)
    cond_x = jax.nn.silu(c)
    cond_z = jax.nn.silu(c_ctx)
    z = ctx
    for l in range(DEPTH):
        last = l == DEPTH - 1
        mx = [m[:, None, :] for m in jnp.split(cond_x @ ada_w[l] + ada_b[l], N_MOD, axis=-1)]
        mz = [m[None, None, :] for m in jnp.split(cond_z @ ada_w[l] + ada_b[l], N_MOD, axis=-1)]
        x = x + 0.5 * mx[2] * swiglu(adaln(x, ffn1_norm[l], mx[0], mx[1]), ffn1_w_in[l], ffn1_w_out[l])
        z = z + 0.5 * mz[2] * swiglu(adaln(z, ffn1_norm[l], mz[0], mz[1]), ffn1_w_in[l], ffn1_w_out[l])
        lp = {
            "w_in": mix_w_in[l], "a_q_norm": a_q_norm[l], "a_k_norm": a_k_norm[l], "a_sink": a_sink[l],
            "b_q_norm": b_q_norm[l], "b_k_norm": b_k_norm[l], "c_q_lat_norm": c_q_lat_norm[l],
            "c_w_uq": c_w_uq[l], "c_kv_lat_norm": c_kv_lat_norm[l], "c_w_ukv": c_w_ukv[l],
            "c_q_norm": c_q_norm[l], "c_k_norm": c_k_norm[l], "w_branch_a": w_branch_a[l],
            "w_branch_b": w_branch_b[l], "w_branch_c": w_branch_c[l], "w_out": mix_w_out[l],
        }
        mix_x, mix_z = token_mixers(adaln(x, mix_norm[l], mx[3], mx[4]), adaln(z, mix_norm[l], mz[3], mz[4]),
                                    lp, rope_hd, rope_c, not last)
        x = x + mx[5] * mix_x
        x = x + 0.5 * mx[8] * swiglu(adaln(x, ffn2_norm[l], mx[6], mx[7]), ffn2_w_in[l], ffn2_w_out[l])
        if not last:
            z = z + mz[5] * mix_z
            z = z + 0.5 * mz[8] * swiglu(adaln(z, ffn2_norm[l], mz[6], mz[7]), ffn2_w_in[l], ffn2_w_out[l])
    return x
```

```python
import functools

import numpy as np
import jax
import jax.numpy as jnp
from jax import lax
from jax.experimental import pallas as pl
from jax.experimental.pallas import tpu as pltpu

F32 = jnp.float32
BF16 = jnp.bfloat16

D_MODEL = 1024
BATCH = 8
SEQ = 2048
DEPTH = 2
GRID_W = 64
CTX_LEN = 256
HEAD_DIM = 64
WINDOW = 128
ROPE_THETA = 10000.0
NORM_EPS = 1e-6
N_MOD = 9
D_FF = 2816
N_HEADS = 8
C_NOPE_DIM = 64
C_ROPE_DIM = 32
C_QK_DIM = C_NOPE_DIM + C_ROPE_DIM
C_V_DIM = 64
C_Q_RANK = 384
C_KV_RANK = 256
KV_COLS = 800
Q_COLS = 1408
GATE_COLS = 3 * D_MODEL

LANES = 128
N_LAT = BATCH * SEQ
N_CTX = BATCH * CTX_LEN
N_ROWS = N_LAT + N_CTX
N_GROUPS = BATCH + 1
GROUP_PAD = 16
CTX_BLK0 = N_LAT // CTX_LEN

VMEM_LIMIT = 56 * 1024 * 1024

PB_AK, PB_AV, PB_BK, PB_BV, PB_CKV, PB_CKR, PB_AQ, PB_BQ, PB_CQ, PB_END = 0, 1, 2, 3, 4, 6, 7, 11, 15, 18
PROJ_COLS = PB_END * LANES

TM_FFN = 512
TF_FFN = 1408
TM_PROJ = 512
TM_MERGE = 512
TQ_DENSE_SHARED = 128
TQ_DENSE_SPLIT = 512
TQ_WINDOW = 256


def _pair_block_cols(off0, off1):
    i = np.arange(32)
    return np.concatenate([off0 + 2 * i, off1 + 2 * i, off0 + 2 * i + 1, off1 + 2 * i + 1])


def _proj_layout():
    src = np.zeros(PROJ_COLS, np.int32)
    keep = np.zeros(PROJ_COLS, np.float32)

    def put(block, cols):
        cols = np.asarray(cols)
        src[block * LANES: block * LANES + cols.size] = cols
        keep[block * LANES: block * LANES + cols.size] = 1.0

    put(PB_AK, _pair_block_cols(0, 64))
    put(PB_AV, 128 + np.arange(128))
    put(PB_BK, _pair_block_cols(256, 320))
    put(PB_BV, 384 + np.arange(128))
    put(PB_CKV, 512 + np.arange(256))
    i = np.arange(16)
    ckr = np.zeros(LANES, np.int32)
    ckr_keep = np.zeros(LANES, np.float32)
    ckr[0:16] = 768 + 2 * i
    ckr[64:80] = 768 + 2 * i + 1
    ckr_keep[0:16] = 1.0
    ckr_keep[64:80] = 1.0
    src[PB_CKR * LANES:(PB_CKR + 1) * LANES] = ckr
    keep[PB_CKR * LANES:(PB_CKR + 1) * LANES] = ckr_keep
    for j in range(4):
        put(PB_AQ + j, _pair_block_cols(KV_COLS + j * 64, KV_COLS + (j + 4) * 64))
        put(PB_BQ + j, _pair_block_cols(KV_COLS + 512 + j * 64, KV_COLS + 512 + (j + 4) * 64))
    put(PB_CQ, KV_COLS + 1024 + np.arange(C_Q_RANK))
    return src, keep


def _c_head_layout(head_stride, nope_off, rope_off):
    src = np.zeros(N_HEADS * LANES, np.int32)
    keep = np.zeros(N_HEADS * LANES, np.float32)
    i = np.arange(16)
    for h in range(N_HEADS):
        b = h * LANES
        base = h * head_stride
        if rope_off is not None:
            src[b:b + 16] = base + rope_off + 2 * i
            keep[b:b + 16] = 1.0
            src[b + 64:b + 80] = base + rope_off + 2 * i + 1
            keep[b + 64:b + 80] = 1.0
        src[b + 16:b + 64] = base + nope_off + np.arange(48)
        keep[b + 16:b + 64] = 1.0
        src[b + 80:b + 96] = base + nope_off + 48 + np.arange(16)
        keep[b + 80:b + 96] = 1.0
    return src, keep


def _gather_cols(w, src, keep):
    return jnp.take(w, jnp.asarray(src), axis=-1) * jnp.asarray(keep)


def _pair_gain(g):
    e, o = g[:, 0::2], g[:, 1::2]
    return jnp.concatenate([e, e, o, o], axis=-1)[:, None, :]


def _c_gain(g):
    src, keep = _c_head_layout(C_QK_DIM, 0, C_NOPE_DIM)
    return _gather_cols(g, src[:LANES], keep[:LANES])[:, None, :]


def _rope_tables(tm):
    pos = np.arange(SEQ)
    row = (pos // GRID_W).astype(np.float32)
    col = (pos % GRID_W).astype(np.float32)

    def angles(rot_dim):
        n_freq = rot_dim // 4
        inv = jnp.asarray(ROPE_THETA, F32) ** (-jnp.arange(n_freq, dtype=F32) / n_freq)
        return jnp.concatenate([jnp.asarray(row)[:, None] * inv, jnp.asarray(col)[:, None] * inv], axis=-1)

    a64 = angles(HEAD_DIM)
    c, s = jnp.cos(a64), jnp.sin(a64)
    cos_ab = jnp.concatenate([c, c, c, c], axis=-1)
    sin_ab = jnp.concatenate([-s, -s, s, s], axis=-1)
    a32 = angles(C_ROPE_DIM)
    c, s = jnp.cos(a32), jnp.sin(a32)
    one48, zero48 = jnp.ones((SEQ, 48), F32), jnp.zeros((SEQ, 48), F32)
    cos_c = jnp.concatenate([c, one48, c, one48], axis=-1)
    sin_c = jnp.concatenate([-s, zero48, s, zero48], axis=-1)
    ident_c, ident_s = jnp.ones((tm, LANES), F32), jnp.zeros((tm, LANES), F32)
    return (jnp.concatenate([cos_ab, ident_c]), jnp.concatenate([sin_ab, ident_s]),
            jnp.concatenate([cos_c, ident_c]), jnp.concatenate([sin_c, ident_s]))


def _adaln_bf16(x, gain, shift, scale):
    y = x * lax.rsqrt(jnp.mean(x * x, axis=-1, keepdims=True) + NORM_EPS) * gain
    return (y * (1.0 + scale) + shift).astype(BF16)


def _rms(x, gain, n):
    return x * lax.rsqrt(jnp.sum(x * x, axis=-1, keepdims=True) * (1.0 / n) + NORM_EPS) * gain


def _rotate(y, cos, sin):
    return y * cos + pltpu.roll(y, 64, axis=1) * sin


def _prep_pair_block(x, gain, cos, sin, head0):
    x2 = x * x
    s0 = jnp.sum(jnp.where(head0, x2, 0.0), axis=-1, keepdims=True)
    s1 = jnp.sum(jnp.where(head0, 0.0, x2), axis=-1, keepdims=True)
    inv0 = lax.rsqrt(s0 * (1.0 / HEAD_DIM) + NORM_EPS)
    inv1 = lax.rsqrt(s1 * (1.0 / HEAD_DIM) + NORM_EPS)
    return _rotate(x * jnp.where(head0, inv0, inv1) * gain, cos, sin)


def _dot(a, b):
    return jnp.dot(a, b, preferred_element_type=F32)


def _dot_nt(a, b):
    return lax.dot_general(a, b, (((1,), (1,)), ((), ())), preferred_element_type=F32)


def _mod_kernel(cond_ref, w_ref, b_ref, o_ref):
    c = cond_ref[...]
    s = (c * jax.nn.sigmoid(c)).astype(BF16)
    o_ref[...] = _dot(s, w_ref[...].astype(BF16)) + b_ref[...]


def _modulation(cond, ada_w, ada_b):
    return pl.pallas_call(
        _mod_kernel,
        grid=(DEPTH, N_MOD),
        in_specs=[
            pl.BlockSpec((GROUP_PAD, D_MODEL), lambda l, k: (0, 0)),
            pl.BlockSpec((None, D_MODEL, D_MODEL), lambda l, k: (l, 0, k)),
            pl.BlockSpec((None, 1, D_MODEL), lambda l, k: (l, 0, k)),
        ],
        out_specs=pl.BlockSpec((None, None, GROUP_PAD, D_MODEL), lambda l, k: (l, k, 0, 0)),
        out_shape=jax.ShapeDtypeStruct((DEPTH, N_MOD, GROUP_PAD, D_MODEL), F32),
        compiler_params=pltpu.CompilerParams(
            dimension_semantics=("arbitrary", "arbitrary"), vmem_limit_bytes=VMEM_LIMIT),
        name="modulation",
    )(cond, ada_w, ada_b.reshape(DEPTH, 1, N_MOD * D_MODEL))


def _mod_spec(layer, k, tm):
    base = (layer * N_MOD + k) * GROUP_PAD
    return pl.BlockSpec((1, 1, D_MODEL), lambda i, *_: (base + (i * tm) // SEQ, 0, 0))


def _ffn_kernel(x_ref, sh_ref, sc_ref, gt_ref, g_ref, wg_ref, wu_ref, wo_ref, o_ref, h_ref, acc_ref):
    j = pl.program_id(1)

    @pl.when(j == 0)
    def _():
        h_ref[...] = _adaln_bf16(x_ref[...], g_ref[...], sh_ref[0], sc_ref[0])

    h = h_ref[...]
    gate = _dot(h, wg_ref[...])
    up = _dot(h, wu_ref[...])
    act = (gate * jax.nn.sigmoid(gate) * up).astype(BF16)
    part = _dot(act, wo_ref[...])

    @pl.when(j == 0)
    def _():
        acc_ref[...] = part

    @pl.when(j > 0)
    def _():
        acc_ref[...] += part

    @pl.when(j == pl.num_programs(1) - 1)
    def _():
        o_ref[...] = x_ref[...] + 0.5 * gt_ref[0] * acc_ref[...]


def _ffn(x, mods, layer, k0, gain, w_in, w_out, n_rows):
    tm, tf = TM_FFN, TF_FFN
    nf = D_FF // tf
    return pl.pallas_call(
        _ffn_kernel,
        grid=(n_rows // tm, nf),
        in_specs=[
            pl.BlockSpec((tm, D_MODEL), lambda i, j: (i, 0)),
            _mod_spec(layer, k0, tm), _mod_spec(layer, k0 + 1, tm), _mod_spec(layer, k0 + 2, tm),
            pl.BlockSpec((None, 1, D_MODEL), lambda i, j: (layer, 0, 0)),
            pl.BlockSpec((None, D_MODEL, tf), lambda i, j: (layer, 0, j)),
            pl.BlockSpec((None, D_MODEL, tf), lambda i, j: (layer, 0, j + nf)),
            pl.BlockSpec((None, tf, D_MODEL), lambda i, j: (layer, j, 0)),
        ],
        out_specs=pl.BlockSpec((tm, D_MODEL), lambda i, j: (i, 0)),
        out_shape=jax.ShapeDtypeStruct((n_rows, D_MODEL), F32),
        scratch_shapes=[pltpu.VMEM((tm, D_MODEL), BF16), pltpu.VMEM((tm, D_MODEL), F32)],
        compiler_params=pltpu.CompilerParams(
            dimension_semantics=("parallel", "arbitrary"), vmem_limit_bytes=VMEM_LIMIT),
        name="ffn",
    )(x, mods, mods, mods, gain, w_in, w_in, w_out)


def _proj_kernel(x_ref, sh_ref, sc_ref, g_ref, wp_ref, wkc_ref, wvc_ref, wqc_ref,
                 gak_ref, gaq_ref, gbk_ref, gbq_ref, gkvl_ref, gql_ref, gck_ref, gcq_ref,
                 cab_ref, sab_ref, cc_ref, sc_c_ref,
                 kvab_ref, qa_ref, qb_ref, qc_ref, kc_ref, vc_ref):
    h = _adaln_bf16(x_ref[...], g_ref[...], sh_ref[0], sc_ref[0])
    raw = _dot(h, wp_ref[...])

    def blk(b, n=1):
        return raw[:, b * LANES:(b + n) * LANES]

    lane = lax.broadcasted_iota(jnp.int32, (1, LANES), 1)
    head0 = (lane & 63) < 32
    cab, sab = cab_ref[...], sab_ref[...]
    qk_scale = HEAD_DIM ** -0.5

    kvab_ref[:, 0 * LANES:1 * LANES] = _prep_pair_block(blk(PB_AK), gak_ref[...], cab, sab, head0).astype(BF16)
    kvab_ref[:, 1 * LANES:2 * LANES] = blk(PB_AV).astype(BF16)
    kvab_ref[:, 2 * LANES:3 * LANES] = _prep_pair_block(blk(PB_BK), gbk_ref[...], cab, sab, head0).astype(BF16)
    kvab_ref[:, 3 * LANES:4 * LANES] = blk(PB_BV).astype(BF16)
    for j in range(4):
        qa_ref[:, j * LANES:(j + 1) * LANES] = _prep_pair_block(
            blk(PB_AQ + j), gaq_ref[...] * qk_scale, cab, sab, head0).astype(BF16)
        qb_ref[:, j * LANES:(j + 1) * LANES] = _prep_pair_block(
            blk(PB_BQ + j), gbq_ref[...] * qk_scale, cab, sab, head0).astype(BF16)

    cc, sc_c = cc_ref[...], sc_c_ref[...]
    ckv = _rms(blk(PB_CKV, 2), gkvl_ref[...], C_KV_RANK).astype(BF16)
    k_nope = _dot(ckv, wkc_ref[...])
    vc_ref[...] = _dot(ckv, wvc_ref[...]).astype(BF16)
    k_rope = blk(PB_CKR)
    cq = _rms(blk(PB_CQ, 3), gql_ref[...], C_Q_RANK).astype(BF16)
    q_c = _dot(cq, wqc_ref[...])
    c_scale = C_QK_DIM ** -0.5
    for hd in range(N_HEADS):
        sl = slice(hd * LANES, (hd + 1) * LANES)
        kh = _rms(k_nope[:, sl] + k_rope, gck_ref[...], C_QK_DIM)
        kc_ref[:, sl] = _rotate(kh, cc, sc_c).astype(BF16)
        qh = _rms(q_c[:, sl], gcq_ref[...] * c_scale, C_QK_DIM)
        qc_ref[:, sl] = _rotate(qh, cc, sc_c).astype(BF16)


def _project(x, mods, layer, gain, wp, wkc, wvc, wqc, gains, tables):
    tm = TM_PROJ
    n_lat_tiles = N_LAT // tm
    tiles_per_seq = SEQ // tm

    def tab_idx(i):
        return (jnp.where(i < n_lat_tiles, i % tiles_per_seq, tiles_per_seq), 0)

    def lvec(n):
        return pl.BlockSpec((None, 1, n), lambda i: (layer, 0, 0))

    def lmat(r, c):
        return pl.BlockSpec((None, r, c), lambda i: (layer, 0, 0))

    def rows(c, dtype=BF16):
        return pl.BlockSpec((tm, c), lambda i: (i, 0)), jax.ShapeDtypeStruct((N_ROWS, c), dtype)

    out = [rows(4 * LANES), rows(4 * LANES), rows(4 * LANES), rows(8 * LANES), rows(8 * LANES), rows(4 * LANES)]
    tab_spec = pl.BlockSpec((tm, LANES), tab_idx)
    return pl.pallas_call(
        _proj_kernel,
        grid=(N_ROWS // tm,),
        in_specs=[
            pl.BlockSpec((tm, D_MODEL), lambda i: (i, 0)),
            _mod_spec(layer, 3, tm), _mod_spec(layer, 4, tm),
            lvec(D_MODEL),
            lmat(D_MODEL, PROJ_COLS), lmat(C_KV_RANK, 8 * LANES), lmat(C_KV_RANK, 4 * LANES),
            lmat(C_Q_RANK, 8 * LANES),
            lvec(LANES), lvec(LANES), lvec(LANES), lvec(LANES), lvec(C_KV_RANK), lvec(C_Q_RANK),
            lvec(LANES), lvec(LANES),
            tab_spec, tab_spec, tab_spec, tab_spec,
        ],
        out_specs=[o[0] for o in out],
        out_shape=[o[1] for o in out],
        compiler_params=pltpu.CompilerParams(
            dimension_semantics=("parallel",), vmem_limit_bytes=VMEM_LIMIT),
        name="project",
    )(x, mods, mods, gain, wp, wkc, wvc, wqc, *gains, *tables)


def _softmax_pv(q, segs, sink_col):
    scores = []
    for k, _, valid in segs:
        s = _dot_nt(q, k)
        if valid is not None:
            s = jnp.where(valid, s, -jnp.inf)
        scores.append(s)
    m = functools.reduce(jnp.maximum, [jnp.max(s, axis=-1, keepdims=True) for s in scores])
    if sink_col is not None:
        m = jnp.maximum(m, sink_col)
    l = None
    o = None
    for s, (_, v, _) in zip(scores, segs):
        p = jnp.exp(s - m)
        ls = jnp.sum(p, axis=-1, keepdims=True)
        os_ = _dot(p.astype(BF16), v)
        l = ls if l is None else l + ls
        o = os_ if o is None else o + os_
    if sink_col is not None:
        l = l + jnp.exp(sink_col - m)
    return o / l


def _stack_pair_heads(q_ref, tq):
    lane = lax.broadcasted_iota(jnp.int32, (1, LANES), 1)
    head0 = (lane & 63) < 32
    zero = jnp.zeros((), BF16)
    blocks = [q_ref[:, j * LANES:(j + 1) * LANES] for j in range(4)]
    return jnp.concatenate([jnp.where(head0, b, zero) for b in blocks]
                           + [jnp.where(head0, zero, b) for b in blocks], axis=0)


def _sink_column(sink_ref, layer, tq):
    return jnp.concatenate([jnp.full((tq, 1), sink_ref[layer, h], F32) for h in range(N_HEADS)], axis=0)


def _store_paired(o_ref, o, tq, first, second):
    low = lax.broadcasted_iota(jnp.int32, (1, LANES), 1) < 64
    for j in range(4):
        a = o[first(j) * tq:(first(j) + 1) * tq]
        b = o[second(j) * tq:(second(j) + 1) * tq]
        o_ref[:, j * LANES:(j + 1) * LANES] = jnp.where(low, a, b).astype(o_ref.dtype)


def _dense_shared_kernel(*refs, layer, tq, n_seg, use_sink):
    q_ref = refs[0]
    kv = refs[1:1 + 2 * n_seg]
    pos = 1 + 2 * n_seg
    sink_ref = refs[pos] if use_sink else None
    o_ref = refs[-1]
    q = _stack_pair_heads(q_ref, tq)
    segs = [(kv[2 * s][...], kv[2 * s + 1][...], None) for s in range(n_seg)]
    sink_col = _sink_column(sink_ref, layer, tq) if use_sink else None
    o = _softmax_pv(q, segs, sink_col)
    _store_paired(o_ref, o, tq, lambda j: j, lambda j: j + 4)


def _dense_split_kernel(*refs, tq, n_seg):
    q_ref = refs[0]
    kv = refs[1:1 + 2 * n_seg]
    o_ref = refs[-1]
    low = lax.broadcasted_iota(jnp.int32, (1, LANES), 1) < 64
    outs = []
    for h in range(N_HEADS):
        ksl = slice(h * LANES, (h + 1) * LANES)
        vsl = slice((h // 2) * LANES, (h // 2 + 1) * LANES)
        segs = [(kv[2 * s][:, ksl], kv[2 * s + 1][:, vsl], None) for s in range(n_seg)]
        outs.append(_softmax_pv(q_ref[:, ksl], segs, None))
    for j in range(4):
        o_ref[:, j * LANES:(j + 1) * LANES] = jnp.where(low, outs[2 * j], outs[2 * j + 1]).astype(o_ref.dtype)


def _window_kernel(q_ref, kl_ref, vl_ref, kc_ref, vc_ref, sink_ref, o_ref, *, layer, tq):
    span = tq + 2 * WINDOW
    start = pl.program_id(1) * tq
    lo = pl.multiple_of(jnp.clip(start - WINDOW, 0, SEQ - span), WINDOW)
    q = _stack_pair_heads(q_ref, tq)
    k_loc = kl_ref[pl.ds(lo, span), :]
    v_loc = vl_ref[pl.ds(lo, span), :]
    q_pos = start + (lax.broadcasted_iota(jnp.int32, (N_HEADS * tq, span), 0) & (tq - 1))
    k_pos = lo + lax.broadcasted_iota(jnp.int32, (N_HEADS * tq, span), 1)
    valid = jnp.abs(q_pos - k_pos) <= WINDOW
    segs = [(k_loc, v_loc, valid), (kc_ref[...], vc_ref[...], None)]
    o = _softmax_pv(q, segs, _sink_column(sink_ref, layer, tq))
    _store_paired(o_ref, o, tq, lambda j: j, lambda j: j + 4)


_SMEM_SPEC = pl.BlockSpec(memory_space=pltpu.SMEM)
_ANY_SPEC = pl.BlockSpec(memory_space=pl.ANY)


def _attn_params(n_grid):
    return pltpu.CompilerParams(dimension_semantics=("parallel",) * n_grid, vmem_limit_bytes=VMEM_LIMIT)


def _latent_dense_shared(q, kvab, k_blk, v_blk, layer, name, n_out):
    tq = TQ_DENSE_SHARED
    nq = SEQ // tq
    kern = functools.partial(_dense_shared_kernel, layer=layer, tq=tq, n_seg=2, use_sink=False)
    return pl.pallas_call(
        kern,
        grid=(BATCH, nq),
        in_specs=[
            pl.BlockSpec((tq, 4 * LANES), lambda b, i: (b * nq + i, 0)),
            pl.BlockSpec((SEQ, LANES), lambda b, i: (b, k_blk)),
            pl.BlockSpec((SEQ, LANES), lambda b, i: (b, v_blk)),
            pl.BlockSpec((CTX_LEN, LANES), lambda b, i: (CTX_BLK0 + b, k_blk)),
            pl.BlockSpec((CTX_LEN, LANES), lambda b, i: (CTX_BLK0 + b, v_blk)),
        ],
        out_specs=pl.BlockSpec((tq, 4 * LANES), lambda b, i: (b * nq + i, 0)),
        out_shape=jax.ShapeDtypeStruct((n_out, 4 * LANES), BF16),
        compiler_params=_attn_params(2),
        name=name,
    )(q, kvab, kvab, kvab, kvab)


def _latent_window(q, kvab, sink, layer, n_out):
    tq = TQ_WINDOW
    nq = SEQ // tq
    kern = functools.partial(_window_kernel, layer=layer, tq=tq)
    return pl.pallas_call(
        kern,
        grid=(BATCH, nq),
        in_specs=[
            pl.BlockSpec((tq, 4 * LANES), lambda b, i: (b * nq + i, 0)),
            pl.BlockSpec((SEQ, LANES), lambda b, i: (b, 0)),
            pl.BlockSpec((SEQ, LANES), lambda b, i: (b, 1)),
            pl.BlockSpec((CTX_LEN, LANES), lambda b, i: (CTX_BLK0 + b, 0)),
            pl.BlockSpec((CTX_LEN, LANES), lambda b, i: (CTX_BLK0 + b, 1)),
            _SMEM_SPEC,
        ],
        out_specs=pl.BlockSpec((tq, 4 * LANES), lambda b, i: (b * nq + i, 0)),
        out_shape=jax.ShapeDtypeStruct((n_out, 4 * LANES), BF16),
        compiler_params=_attn_params(2),
        name="attn_a_window",
    )(q, kvab, kvab, kvab, kvab, sink)


def _latent_dense_split(q, k, v, n_out):
    tq = TQ_DENSE_SPLIT
    nq = SEQ // tq
    kern = functools.partial(_dense_split_kernel, tq=tq, n_seg=2)
    return pl.pallas_call(
        kern,
        grid=(BATCH, nq),
        in_specs=[
            pl.BlockSpec((tq, 8 * LANES), lambda b, i: (b * nq + i, 0)),
            pl.BlockSpec((SEQ, 8 * LANES), lambda b, i: (b, 0)),
            pl.BlockSpec((SEQ, 4 * LANES), lambda b, i: (b, 0)),
            pl.BlockSpec((CTX_LEN, 8 * LANES), lambda b, i: (CTX_BLK0 + b, 0)),
            pl.BlockSpec((CTX_LEN, 4 * LANES), lambda b, i: (CTX_BLK0 + b, 0)),
        ],
        out_specs=pl.BlockSpec((tq, 4 * LANES), lambda b, i: (b * nq + i, 0)),
        out_shape=jax.ShapeDtypeStruct((n_out, 4 * LANES), BF16),
        compiler_params=_attn_params(2),
        name="attn_c_latent",
    )(q, k, v, k, v)


def _context_shared(y, q, kvab, k_blk, v_blk, sink, layer, name):
    tq = CTX_LEN
    use_sink = sink is not None
    kern = functools.partial(_dense_shared_kernel, layer=layer, tq=tq, n_seg=1, use_sink=use_sink)
    in_specs = [
        pl.BlockSpec((tq, 4 * LANES), lambda b: (CTX_BLK0 + b, 0)),
        pl.BlockSpec((CTX_LEN, LANES), lambda b: (CTX_BLK0 + b, k_blk)),
        pl.BlockSpec((CTX_LEN, LANES), lambda b: (CTX_BLK0 + b, v_blk)),
    ]
    args = [q, kvab, kvab]
    if use_sink:
        in_specs.append(_SMEM_SPEC)
        args.append(sink)
    in_specs.append(_ANY_SPEC)
    args.append(y)
    return pl.pallas_call(
        kern,
        grid=(BATCH,),
        in_specs=in_specs,
        out_specs=pl.BlockSpec((tq, 4 * LANES), lambda b: (CTX_BLK0 + b, 0)),
        out_shape=jax.ShapeDtypeStruct((N_ROWS, 4 * LANES), BF16),
        input_output_aliases={len(args) - 1: 0},
        compiler_params=_attn_params(1),
        name=name,
    )(*args)


def _context_split(y, q, k, v):
    tq = CTX_LEN
    kern = functools.partial(_dense_split_kernel, tq=tq, n_seg=1)
    return pl.pallas_call(
        kern,
        grid=(BATCH,),
        in_specs=[
            pl.BlockSpec((tq, 8 * LANES), lambda b: (CTX_BLK0 + b, 0)),
            pl.BlockSpec((CTX_LEN, 8 * LANES), lambda b: (CTX_BLK0 + b, 0)),
            pl.BlockSpec((CTX_LEN, 4 * LANES), lambda b: (CTX_BLK0 + b, 0)),
            _ANY_SPEC,
        ],
        out_specs=pl.BlockSpec((tq, 4 * LANES), lambda b: (CTX_BLK0 + b, 0)),
        out_shape=jax.ShapeDtypeStruct((N_ROWS, 4 * LANES), BF16),
        input_output_aliases={3: 0},
        compiler_params=_attn_params(1),
        name="attn_c_context",
    )(q, k, v, y)


def _merge_kernel(x_ref, sh_ref, sc_ref, gt_ref, g_ref, wg_ref, ya_ref, yb_ref, yc_ref,
                  wa_ref, wb_ref, wc_ref, wo_ref, o_ref):
    x = x_ref[...]
    u = _adaln_bf16(x, g_ref[...], sh_ref[0], sc_ref[0])
    m = None
    for idx, (y_ref, w_ref) in enumerate(((ya_ref, wa_ref), (yb_ref, wb_ref), (yc_ref, wc_ref))):
        gate = _dot(u, wg_ref[:, idx * D_MODEL:(idx + 1) * D_MODEL])
        term = jax.nn.sigmoid(gate) * _dot(y_ref[...], w_ref[...])
        m = term if m is None else m + term
    o_ref[...] = x + gt_ref[0] * _dot(m.astype(BF16), wo_ref[...])


def _merge(x, mods, layer, gain, wg, ya, yb, yc, wa, wb, wc, wo, n_rows):
    tm = TM_MERGE

    def lmat(r, c):
        return pl.BlockSpec((None, r, c), lambda i: (layer, 0, 0))

    y_spec = pl.BlockSpec((tm, 4 * LANES), lambda i: (i, 0))
    return pl.pallas_call(
        _merge_kernel,
        grid=(n_rows // tm,),
        in_specs=[
            pl.BlockSpec((tm, D_MODEL), lambda i: (i, 0)),
            _mod_spec(layer, 3, tm), _mod_spec(layer, 4, tm), _mod_spec(layer, 5, tm),
            pl.BlockSpec((None, 1, D_MODEL), lambda i: (layer, 0, 0)),
            lmat(D_MODEL, GATE_COLS),
            y_spec, y_spec, y_spec,
            lmat(4 * LANES, D_MODEL), lmat(4 * LANES, D_MODEL), lmat(4 * LANES, D_MODEL),
            lmat(D_MODEL, D_MODEL),
        ],
        out_specs=pl.BlockSpec((tm, D_MODEL), lambda i: (i, 0)),
        out_shape=jax.ShapeDtypeStruct((n_rows, D_MODEL), F32),
        compiler_params=pltpu.CompilerParams(
            dimension_semantics=("parallel",), vmem_limit_bytes=VMEM_LIMIT),
        name="merge",
    )(x, mods, mods, mods, gain, wg, ya, yb, yc, wa, wb, wc, wo)


def kernel(x, c, ctx, c_ctx, ada_w, ada_b, ffn1_norm, ffn1_w_in, ffn1_w_out, mix_norm, mix_w_in, a_q_norm, a_k_norm, a_sink, b_q_norm, b_k_norm, c_q_lat_norm, c_w_uq, c_kv_lat_norm, c_w_ukv, c_q_norm, c_k_norm, w_branch_a, w_branch_b, w_branch_c, mix_w_out, ffn2_norm, ffn2_w_in, ffn2_w_out):
    src, keep = _proj_layout()
    wp = _gather_cols(mix_w_in, src, keep).astype(BF16)
    wg = mix_w_in[:, :, KV_COLS + Q_COLS:].astype(BF16)
    src, keep = _c_head_layout(C_NOPE_DIM + C_V_DIM, 0, None)
    wkc = _gather_cols(c_w_ukv, src, keep).astype(BF16)
    v_cols = (np.arange(N_HEADS)[:, None] * (C_NOPE_DIM + C_V_DIM) + C_NOPE_DIM + np.arange(C_V_DIM)).reshape(-1)
    wvc = jnp.take(c_w_ukv, jnp.asarray(v_cols), axis=-1).astype(BF16)
    src, keep = _c_head_layout(C_QK_DIM, 0, C_NOPE_DIM)
    wqc = _gather_cols(c_w_uq, src, keep).astype(BF16)
    pair_rows = np.concatenate([np.concatenate([j * 64 + np.arange(64), (j + 4) * 64 + np.arange(64)])
                                for j in range(4)])
    wa = jnp.take(w_branch_a, jnp.asarray(pair_rows), axis=1).astype(BF16)
    wb = jnp.take(w_branch_b, jnp.asarray(pair_rows), axis=1).astype(BF16)
    wc = w_branch_c.astype(BF16)
    wo = mix_w_out.astype(BF16)
    f1_in, f1_out = ffn1_w_in.astype(BF16), ffn1_w_out.astype(BF16)
    f2_in, f2_out = ffn2_w_in.astype(BF16), ffn2_w_out.astype(BF16)
    gains = (_pair_gain(a_k_norm), _pair_gain(a_q_norm), _pair_gain(b_k_norm), _pair_gain(b_q_norm),
             c_kv_lat_norm[:, None, :], c_q_lat_norm[:, None, :], _c_gain(c_k_norm), _c_gain(c_q_norm))
    tables = _rope_tables(TM_PROJ)
    ffn1_g, mix_g, ffn2_g = ffn1_norm[:, None, :], mix_norm[:, None, :], ffn2_norm[:, None, :]

    cond = jnp.concatenate([c, c_ctx[None, :], jnp.zeros((GROUP_PAD - N_GROUPS, D_MODEL), F32)], axis=0)
    mods = _modulation(cond, ada_w, ada_b).reshape(DEPTH * N_MOD * GROUP_PAD, 1, D_MODEL)

    xs = jnp.concatenate([x.reshape(N_LAT, D_MODEL), ctx.reshape(N_CTX, D_MODEL)], axis=0)
    for l in range(DEPTH):
        last = l == DEPTH - 1
        xs = _ffn(xs, mods, l, 0, ffn1_g, f1_in, f1_out, xs.shape[0])
        kvab, qa, qb, qc, kc, vc = _project(xs, mods, l, mix_g, wp, wkc, wvc, wqc, gains, tables)
        n_rows = N_LAT if last else N_ROWS
        ya = _latent_window(qa, kvab, a_sink, l, n_rows)
        yb = _latent_dense_shared(qb, kvab, 2, 3, l, "attn_b_latent", n_rows)
        yc = _latent_dense_split(qc, kc, vc, n_rows)
        if not last:
            ya = _context_shared(ya, qa, kvab, 0, 1, a_sink, l, "attn_a_context")
            yb = _context_shared(yb, qb, kvab, 2, 3, None, l, "attn_b_context")
            yc = _context_split(yc, qc, kc, vc)
        xs = _merge(xs, mods, l, mix_g, wg, ya, yb, yc, wa, wb, wc, wo, n_rows)
        xs = _ffn(xs, mods, l, 6, ffn2_g, f2_in, f2_out, n_rows)
    return xs.reshape(BATCH, SEQ, D_MODEL)
```

```python
import functools
import math

import numpy as np
import jax
import jax.numpy as jnp
from jax import lax
from jax.experimental import pallas as pl
from jax.experimental.pallas import tpu as pltpu

F32 = jnp.float32
BF16 = jnp.bfloat16

D_MODEL = 1024
BATCH = 8
SEQ = 2048
DEPTH = 2
GRID_W = 64
CTX_LEN = 256
HEAD_DIM = 64
WINDOW = 128
ROPE_THETA = 10000.0
NORM_EPS = 1e-6
N_MOD = 9
D_FF = 2816
N_HEADS = 8
C_NOPE_DIM = 64
C_ROPE_DIM = 32
C_QK_DIM = C_NOPE_DIM + C_ROPE_DIM
C_V_DIM = 64
C_Q_RANK = 384
C_KV_RANK = 256
KV_COLS = 800
Q_COLS = 1408
GATE_COLS = 3 * D_MODEL

LANES = 128
N_LAT = BATCH * SEQ
N_CTX = BATCH * CTX_LEN
N_ROWS = N_LAT + N_CTX
N_GROUPS = BATCH + 1
GROUP_PAD = 16
CTX_BLK0 = N_LAT // CTX_LEN

VMEM_LIMIT = 56 * 1024 * 1024
LOG2E = math.log2(math.e)

PB_AK, PB_AV, PB_BK, PB_BV, PB_CKV, PB_CKR, PB_AQ, PB_BQ, PB_CQ, PB_END = 0, 1, 2, 3, 4, 6, 7, 11, 15, 18
PROJ_COLS = PB_END * LANES

TM_FFN = 512
TF_FFN = 1408
TM_PROJ = 512
TM_MERGE = 512
TQ_DENSE_SHARED = 512
TQ_DENSE_SPLIT = 512
TQ_WINDOW = 256

_SMEM_SPEC = pl.BlockSpec(memory_space=pltpu.SMEM)
_ANY_SPEC = pl.BlockSpec(memory_space=pl.ANY)


def _pair_block(w, off0, off1):
    return jnp.concatenate([w[..., off0:off0 + 64:2], w[..., off1:off1 + 64:2],
                            w[..., off0 + 1:off0 + 64:2], w[..., off1 + 1:off1 + 64:2]], axis=-1)


def _proj_weight(w_in):
    zeros48 = jnp.zeros(w_in.shape[:-1] + (48,), w_in.dtype)
    q0 = KV_COLS
    blocks = [_pair_block(w_in, 0, 64), w_in[..., 128:256], _pair_block(w_in, 256, 320), w_in[..., 384:512],
              w_in[..., 512:768],
              w_in[..., 768:800:2], zeros48, w_in[..., 769:800:2], zeros48]
    blocks += [_pair_block(w_in, q0 + j * 64, q0 + (j + 4) * 64) for j in range(4)]
    blocks += [_pair_block(w_in, q0 + 512 + j * 64, q0 + 512 + (j + 4) * 64) for j in range(4)]
    blocks.append(w_in[..., q0 + 1024:q0 + 1024 + C_Q_RANK])
    return jnp.concatenate(blocks, axis=-1).astype(BF16)


def _c_head_blocks(nope, rope):
    lead = nope.shape[:-1]
    if rope is None:
        r_even = r_odd = jnp.zeros(lead + (16,), nope.dtype)
    else:
        r_even, r_odd = rope[..., 0::2], rope[..., 1::2]
    blk = jnp.concatenate([r_even, nope[..., :48], r_odd, nope[..., 48:], jnp.zeros(lead + (32,), nope.dtype)],
                          axis=-1)
    return blk.reshape(lead[:-1] + (N_HEADS * LANES,))


def _pair_gain(g):
    e, o = g[:, 0::2], g[:, 1::2]
    return jnp.concatenate([e, e, o, o], axis=-1)[:, None, :]


def _c_gain(g):
    nope, rope = g[:, :C_NOPE_DIM], g[:, C_NOPE_DIM:]
    return jnp.concatenate([rope[:, 0::2], nope[:, :48], rope[:, 1::2], nope[:, 48:],
                            jnp.zeros((g.shape[0], 32), g.dtype)], axis=-1)[:, None, :]


def _rope_tables(tm):
    pos = np.arange(SEQ)
    row = (pos // GRID_W).astype(np.float32)
    col = (pos % GRID_W).astype(np.float32)

    def angles(rot_dim):
        n_freq = rot_dim // 4
        inv = jnp.asarray(ROPE_THETA, F32) ** (-jnp.arange(n_freq, dtype=F32) / n_freq)
        return jnp.concatenate([jnp.asarray(row)[:, None] * inv, jnp.asarray(col)[:, None] * inv], axis=-1)

    a64 = angles(HEAD_DIM)
    c, s = jnp.cos(a64), jnp.sin(a64)
    cos_ab = jnp.concatenate([c, c, c, c], axis=-1)
    sin_ab = jnp.concatenate([-s, -s, s, s], axis=-1)
    a32 = angles(C_ROPE_DIM)
    c, s = jnp.cos(a32), jnp.sin(a32)
    one48, zero48 = jnp.ones((SEQ, 48), F32), jnp.zeros((SEQ, 48), F32)
    cos_c = jnp.concatenate([c, one48, c, one48], axis=-1)
    sin_c = jnp.concatenate([-s, zero48, s, zero48], axis=-1)
    ident_c, ident_s = jnp.ones((tm, LANES), F32), jnp.zeros((tm, LANES), F32)
    return (jnp.concatenate([cos_ab, ident_c]), jnp.concatenate([sin_ab, ident_s]),
            jnp.concatenate([cos_c, ident_c]), jnp.concatenate([sin_c, ident_s]))


def _adaln_bf16(x, gain, shift, scale):
    y = x * lax.rsqrt(jnp.mean(x * x, axis=-1, keepdims=True) + NORM_EPS) * gain
    return (y * (1.0 + scale) + shift).astype(BF16)


def _rms(x, gain, n):
    return x * lax.rsqrt(jnp.sum(x * x, axis=-1, keepdims=True) * (1.0 / n) + NORM_EPS) * gain


def _rotate(y, cos, sin):
    return y * cos + pltpu.roll(y, 64, axis=1) * sin


def _prep_pair_block(x, gain, cos, sin, head0):
    x2 = x * x
    s0 = jnp.sum(jnp.where(head0, x2, 0.0), axis=-1, keepdims=True)
    s1 = jnp.sum(jnp.where(head0, 0.0, x2), axis=-1, keepdims=True)
    inv0 = lax.rsqrt(s0 * (1.0 / HEAD_DIM) + NORM_EPS)
    inv1 = lax.rsqrt(s1 * (1.0 / HEAD_DIM) + NORM_EPS)
    return _rotate(x * jnp.where(head0, inv0, inv1) * gain, cos, sin)


def _dot(a, b):
    return jnp.dot(a, b, preferred_element_type=F32)


def _dot_nt(a, b):
    return lax.dot_general(a, b, (((1,), (1,)), ((), ())), preferred_element_type=F32)


def _mod_kernel(cond_ref, w_ref, b_ref, o_ref):
    c = cond_ref[...]
    s = (c * jax.nn.sigmoid(c)).astype(BF16)
    o_ref[...] = _dot(s, w_ref[...].astype(BF16)) + b_ref[...]


def _modulation(cond, ada_w, ada_b):
    return pl.pallas_call(
        _mod_kernel,
        grid=(DEPTH, N_MOD),
        in_specs=[
            pl.BlockSpec((GROUP_PAD, D_MODEL), lambda l, k: (0, 0)),
            pl.BlockSpec((None, D_MODEL, D_MODEL), lambda l, k: (l, 0, k)),
            pl.BlockSpec((None, 1, D_MODEL), lambda l, k: (l, 0, k)),
        ],
        out_specs=pl.BlockSpec((None, None, GROUP_PAD, D_MODEL), lambda l, k: (l, k, 0, 0)),
        out_shape=jax.ShapeDtypeStruct((DEPTH, N_MOD, GROUP_PAD, D_MODEL), F32),
        compiler_params=pltpu.CompilerParams(
            dimension_semantics=("arbitrary", "arbitrary"), vmem_limit_bytes=VMEM_LIMIT),
        name="modulation",
    )(cond, ada_w, ada_b.reshape(DEPTH, 1, N_MOD * D_MODEL))


def _mod_spec(layer, k, tm, row0=0):
    base = (layer * N_MOD + k) * GROUP_PAD
    return pl.BlockSpec((1, 1, D_MODEL), lambda i, *_: (base + (row0 + i * tm) // SEQ, 0, 0))


def _ffn_kernel(x_ref, sh_ref, sc_ref, gt_ref, g_ref, wg_ref, wu_ref, wo_ref, o_ref, h_ref, acc_ref):
    j = pl.program_id(1)

    @pl.when(j == 0)
    def _():
        h_ref[...] = _adaln_bf16(x_ref[...], g_ref[...], sh_ref[0], sc_ref[0])

    h = h_ref[...]
    gate = _dot(h, wg_ref[...])
    up = _dot(h, wu_ref[...])
    act = (gate * jax.nn.sigmoid(gate) * up).astype(BF16)
    part = _dot(act, wo_ref[...])

    @pl.when(j == 0)
    def _():
        acc_ref[...] = part

    @pl.when(j > 0)
    def _():
        acc_ref[...] += part

    @pl.when(j == pl.num_programs(1) - 1)
    def _():
        o_ref[...] = x_ref[...] + 0.5 * gt_ref[0] * acc_ref[...]


def _ffn_kernel_aliased(x_ref, sh_ref, sc_ref, gt_ref, g_ref, wg_ref, wu_ref, wo_ref, _, o_ref, h_ref, acc_ref):
    _ffn_kernel(x_ref, sh_ref, sc_ref, gt_ref, g_ref, wg_ref, wu_ref, wo_ref, o_ref, h_ref, acc_ref)


def _ffn(x, mods, layer, k0, gain, w_in, w_out, n_out, row0=0, out_buf=None):
    tm, tf = TM_FFN, TF_FFN
    nf = D_FF // tf
    blk0 = row0 // tm
    in_specs = [
        pl.BlockSpec((tm, D_MODEL), lambda i, j: (i, 0)),
        _mod_spec(layer, k0, tm, row0), _mod_spec(layer, k0 + 1, tm, row0), _mod_spec(layer, k0 + 2, tm, row0),
        pl.BlockSpec((None, 1, D_MODEL), lambda i, j: (layer, 0, 0)),
        pl.BlockSpec((None, D_MODEL, tf), lambda i, j: (layer, 0, j)),
        pl.BlockSpec((None, D_MODEL, tf), lambda i, j: (layer, 0, j + nf)),
        pl.BlockSpec((None, tf, D_MODEL), lambda i, j: (layer, j, 0)),
    ]
    args = [x, mods, mods, mods, gain, w_in, w_in, w_out]
    aliases = {}
    kern = _ffn_kernel
    if out_buf is not None:
        in_specs.append(_ANY_SPEC)
        args.append(out_buf)
        aliases = {len(args) - 1: 0}
        kern = _ffn_kernel_aliased
    return pl.pallas_call(
        kern,
        grid=(x.shape[0] // tm, nf),
        in_specs=in_specs,
        out_specs=pl.BlockSpec((tm, D_MODEL), lambda i, j: (blk0 + i, 0)),
        out_shape=jax.ShapeDtypeStruct((n_out, D_MODEL), F32),
        scratch_shapes=[pltpu.VMEM((tm, D_MODEL), BF16), pltpu.VMEM((tm, D_MODEL), F32)],
        input_output_aliases=aliases,
        compiler_params=pltpu.CompilerParams(
            dimension_semantics=("parallel", "arbitrary"), vmem_limit_bytes=VMEM_LIMIT),
        name="ffn",
    )(*args)


def _proj_kernel(x_ref, sh_ref, sc_ref, g_ref, wp_ref, wkc_ref, wvc_ref, wqc_ref,
                 gak_ref, gaq_ref, gbk_ref, gbq_ref, gkvl_ref, gql_ref, gck_ref, gcq_ref,
                 cab_ref, sab_ref, cc_ref, sc_c_ref,
                 kvab_ref, qa_ref, qb_ref, qc_ref, kc_ref, vc_ref):
    h = _adaln_bf16(x_ref[...], g_ref[...], sh_ref[0], sc_ref[0])
    raw = _dot(h, wp_ref[...])

    def blk(b, n=1):
        return raw[:, b * LANES:(b + n) * LANES]

    lane = lax.broadcasted_iota(jnp.int32, (1, LANES), 1)
    head0 = (lane & 63) < 32
    cab, sab = cab_ref[...], sab_ref[...]
    qk_scale = HEAD_DIM ** -0.5 * LOG2E

    kvab_ref[:, 0 * LANES:1 * LANES] = _prep_pair_block(blk(PB_AK), gak_ref[...], cab, sab, head0).astype(BF16)
    kvab_ref[:, 1 * LANES:2 * LANES] = blk(PB_AV).astype(BF16)
    kvab_ref[:, 2 * LANES:3 * LANES] = _prep_pair_block(blk(PB_BK), gbk_ref[...], cab, sab, head0).astype(BF16)
    kvab_ref[:, 3 * LANES:4 * LANES] = blk(PB_BV).astype(BF16)
    for j in range(4):
        qa_ref[:, j * LANES:(j + 1) * LANES] = _prep_pair_block(
            blk(PB_AQ + j), gaq_ref[...] * qk_scale, cab, sab, head0).astype(BF16)
        qb_ref[:, j * LANES:(j + 1) * LANES] = _prep_pair_block(
            blk(PB_BQ + j), gbq_ref[...] * qk_scale, cab, sab, head0).astype(BF16)

    cc, sc_c = cc_ref[...], sc_c_ref[...]
    ckv = _rms(blk(PB_CKV, 2), gkvl_ref[...], C_KV_RANK).astype(BF16)
    k_nope = _dot(ckv, wkc_ref[...])
    vc_ref[...] = _dot(ckv, wvc_ref[...]).astype(BF16)
    k_rope = blk(PB_CKR)
    cq = _rms(blk(PB_CQ, 3), gql_ref[...], C_Q_RANK).astype(BF16)
    q_c = _dot(cq, wqc_ref[...])
    c_scale = C_QK_DIM ** -0.5 * LOG2E
    for hd in range(N_HEADS):
        sl = slice(hd * LANES, (hd + 1) * LANES)
        kh = _rms(k_nope[:, sl] + k_rope, gck_ref[...], C_QK_DIM)
        kc_ref[:, sl] = _rotate(kh, cc, sc_c).astype(BF16)
        qh = _rms(q_c[:, sl], gcq_ref[...] * c_scale, C_QK_DIM)
        qc_ref[:, sl] = _rotate(qh, cc, sc_c).astype(BF16)


def _project(x, mods, layer, gain, wp, wkc, wvc, wqc, gains, tables):
    tm = TM_PROJ
    n_lat_tiles = N_LAT // tm
    tiles_per_seq = SEQ // tm

    def tab_idx(i):
        return (jnp.where(i < n_lat_tiles, i % tiles_per_seq, tiles_per_seq), 0)

    def lvec(n):
        return pl.BlockSpec((None, 1, n), lambda i: (layer, 0, 0))

    def lmat(r, c):
        return pl.BlockSpec((None, r, c), lambda i: (layer, 0, 0))

    def rows(c, dtype=BF16):
        return pl.BlockSpec((tm, c), lambda i: (i, 0)), jax.ShapeDtypeStruct((N_ROWS, c), dtype)

    out = [rows(4 * LANES), rows(4 * LANES), rows(4 * LANES), rows(8 * LANES), rows(8 * LANES), rows(4 * LANES)]
    tab_spec = pl.BlockSpec((tm, LANES), tab_idx)
    return pl.pallas_call(
        _proj_kernel,
        grid=(N_ROWS // tm,),
        in_specs=[
            pl.BlockSpec((tm, D_MODEL), lambda i: (i, 0)),
            _mod_spec(layer, 3, tm), _mod_spec(layer, 4, tm),
            lvec(D_MODEL),
            lmat(D_MODEL, PROJ_COLS), lmat(C_KV_RANK, 8 * LANES), lmat(C_KV_RANK, 4 * LANES),
            lmat(C_Q_RANK, 8 * LANES),
            lvec(LANES), lvec(LANES), lvec(LANES), lvec(LANES), lvec(C_KV_RANK), lvec(C_Q_RANK),
            lvec(LANES), lvec(LANES),
            tab_spec, tab_spec, tab_spec, tab_spec,
        ],
        out_specs=[o[0] for o in out],
        out_shape=[o[1] for o in out],
        compiler_params=pltpu.CompilerParams(
            dimension_semantics=("parallel",), vmem_limit_bytes=VMEM_LIMIT),
        name="project",
    )(x, mods, mods, gain, wp, wkc, wvc, wqc, *gains, *tables)


def _with_ones(v):
    return jnp.concatenate([v, jnp.ones_like(v)], axis=1)


def _attend(q, segs, sink_col):
    scores = []
    for k, _, bias in segs:
        s = _dot_nt(q, k)
        scores.append(s if bias is None else s + bias)
    m = functools.reduce(jnp.maximum, [jnp.max(s, axis=-1, keepdims=True) for s in scores])
    if sink_col is not None:
        m = jnp.maximum(m, sink_col)
    acc = None
    for s, (_, v, _) in zip(scores, segs):
        part = _dot(jnp.exp2(s - m).astype(BF16), v)
        acc = part if acc is None else acc + part
    denom = acc[:, LANES:]
    if sink_col is not None:
        denom = denom + jnp.exp2(sink_col - m)
    return acc[:, :LANES] / denom


def _pair_masks():
    lane = lax.broadcasted_iota(jnp.int32, (1, LANES), 1)
    return (lane & 63) < 32, lane < 64


def _dense_shared_kernel(*refs, layer, tq, n_seg, use_sink):
    q_ref = refs[0]
    kv = refs[1:1 + 2 * n_seg]
    sink_ref = refs[1 + 2 * n_seg] if use_sink else None
    o_ref = refs[-1]
    head0, low = _pair_masks()
    zero = jnp.zeros((), BF16)
    segs = [(kv[2 * s][...], _with_ones(kv[2 * s + 1][...]), None) for s in range(n_seg)]
    for j in range(4):
        qb = q_ref[:, j * LANES:(j + 1) * LANES]
        outs = []
        for half in range(2):
            q = jnp.where(head0, qb, zero) if half == 0 else jnp.where(head0, zero, qb)
            sink_col = (jnp.full((tq, 1), sink_ref[layer, j + 4 * half] * LOG2E, F32) if use_sink else None)
            outs.append(_attend(q, segs, sink_col))
        o_ref[:, j * LANES:(j + 1) * LANES] = jnp.where(low, outs[0], outs[1]).astype(o_ref.dtype)


def _dense_split_kernel(*refs, tq, n_seg):
    q_ref = refs[0]
    kv = refs[1:1 + 2 * n_seg]
    o_ref = refs[-1]
    _, low = _pair_masks()
    for j in range(4):
        vsl = slice(j * LANES, (j + 1) * LANES)
        vals = [_with_ones(kv[2 * s + 1][:, vsl]) for s in range(n_seg)]
        outs = []
        for half in range(2):
            ksl = slice((2 * j + half) * LANES, (2 * j + half + 1) * LANES)
            segs = [(kv[2 * s][:, ksl], vals[s], None) for s in range(n_seg)]
            outs.append(_attend(q_ref[:, ksl], segs, None))
        o_ref[:, vsl] = jnp.where(low, outs[0], outs[1]).astype(o_ref.dtype)


def _window_kernel(q_ref, kl_ref, vl_ref, kc_ref, vc_ref, sink_ref, o_ref, *, layer, tq):
    span = tq + 2 * WINDOW
    start = pl.program_id(1) * tq
    lo = pl.multiple_of(jnp.clip(start - WINDOW, 0, SEQ - span), WINDOW)
    head0, low = _pair_masks()
    zero = jnp.zeros((), BF16)
    q_pos = start + lax.broadcasted_iota(jnp.int32, (tq, span), 0)
    k_pos = lo + lax.broadcasted_iota(jnp.int32, (tq, span), 1)
    band = jnp.where(jnp.abs(q_pos - k_pos) <= WINDOW, 0.0, -jnp.inf)
    band = jnp.concatenate([band, band], axis=0)
    segs = [(kl_ref[pl.ds(lo, span), :], _with_ones(vl_ref[pl.ds(lo, span), :]), band),
            (kc_ref[...], _with_ones(vc_ref[...]), None)]
    for j in range(4):
        qb = q_ref[:, j * LANES:(j + 1) * LANES]
        q = jnp.concatenate([jnp.where(head0, qb, zero), jnp.where(head0, zero, qb)], axis=0)
        sink_col = jnp.concatenate([jnp.full((tq, 1), sink_ref[layer, j] * LOG2E, F32),
                                    jnp.full((tq, 1), sink_ref[layer, j + 4] * LOG2E, F32)], axis=0)
        o = _attend(q, segs, sink_col)
        o_ref[:, j * LANES:(j + 1) * LANES] = jnp.where(low, o[:tq], o[tq:]).astype(o_ref.dtype)


def _attn_params(n_grid):
    return pltpu.CompilerParams(dimension_semantics=("parallel",) * n_grid, vmem_limit_bytes=VMEM_LIMIT)


def _latent_dense_shared(q, kvab, k_blk, v_blk, layer, name, n_out):
    tq = TQ_DENSE_SHARED
    nq = SEQ // tq
    kern = functools.partial(_dense_shared_kernel, layer=layer, tq=tq, n_seg=2, use_sink=False)
    return pl.pallas_call(
        kern,
        grid=(BATCH, nq),
        in_specs=[
            pl.BlockSpec((tq, 4 * LANES), lambda b, i: (b * nq + i, 0)),
            pl.BlockSpec((SEQ, LANES), lambda b, i: (b, k_blk)),
            pl.BlockSpec((SEQ, LANES), lambda b, i: (b, v_blk)),
            pl.BlockSpec((CTX_LEN, LANES), lambda b, i: (CTX_BLK0 + b, k_blk)),
            pl.BlockSpec((CTX_LEN, LANES), lambda b, i: (CTX_BLK0 + b, v_blk)),
        ],
        out_specs=pl.BlockSpec((tq, 4 * LANES), lambda b, i: (b * nq + i, 0)),
        out_shape=jax.ShapeDtypeStruct((n_out, 4 * LANES), BF16),
        compiler_params=_attn_params(2),
        name=name,
    )(q, kvab, kvab, kvab, kvab)


def _latent_window(q, kvab, sink, layer, n_out):
    tq = TQ_WINDOW
    nq = SEQ // tq
    kern = functools.partial(_window_kernel, layer=layer, tq=tq)
    return pl.pallas_call(
        kern,
        grid=(BATCH, nq),
        in_specs=[
            pl.BlockSpec((tq, 4 * LANES), lambda b, i: (b * nq + i, 0)),
            pl.BlockSpec((SEQ, LANES), lambda b, i: (b, 0)),
            pl.BlockSpec((SEQ, LANES), lambda b, i: (b, 1)),
            pl.BlockSpec((CTX_LEN, LANES), lambda b, i: (CTX_BLK0 + b, 0)),
            pl.BlockSpec((CTX_LEN, LANES), lambda b, i: (CTX_BLK0 + b, 1)),
            _SMEM_SPEC,
        ],
        out_specs=pl.BlockSpec((tq, 4 * LANES), lambda b, i: (b * nq + i, 0)),
        out_shape=jax.ShapeDtypeStruct((n_out, 4 * LANES), BF16),
        compiler_params=_attn_params(2),
        name="attn_a_window",
    )(q, kvab, kvab, kvab, kvab, sink)


def _latent_dense_split(q, k, v, n_out):
    tq = TQ_DENSE_SPLIT
    nq = SEQ // tq
    kern = functools.partial(_dense_split_kernel, tq=tq, n_seg=2)
    return pl.pallas_call(
        kern,
        grid=(BATCH, nq),
        in_specs=[
            pl.BlockSpec((tq, 8 * LANES), lambda b, i: (b * nq + i, 0)),
            pl.BlockSpec((SEQ, 8 * LANES), lambda b, i: (b, 0)),
            pl.BlockSpec((SEQ, 4 * LANES), lambda b, i: (b, 0)),
            pl.BlockSpec((CTX_LEN, 8 * LANES), lambda b, i: (CTX_BLK0 + b, 0)),
            pl.BlockSpec((CTX_LEN, 4 * LANES), lambda b, i: (CTX_BLK0 + b, 0)),
        ],
        out_specs=pl.BlockSpec((tq, 4 * LANES), lambda b, i: (b * nq + i, 0)),
        out_shape=jax.ShapeDtypeStruct((n_out, 4 * LANES), BF16),
        compiler_params=_attn_params(2),
        name="attn_c_latent",
    )(q, k, v, k, v)


def _context_shared(y, q, kvab, k_blk, v_blk, sink, layer, name):
    tq = CTX_LEN
    use_sink = sink is not None
    kern = functools.partial(_dense_shared_kernel, layer=layer, tq=tq, n_seg=1, use_sink=use_sink)
    in_specs = [
        pl.BlockSpec((tq, 4 * LANES), lambda b: (CTX_BLK0 + b, 0)),
        pl.BlockSpec((CTX_LEN, LANES), lambda b: (CTX_BLK0 + b, k_blk)),
        pl.BlockSpec((CTX_LEN, LANES), lambda b: (CTX_BLK0 + b, v_blk)),
    ]
    args = [q, kvab, kvab]
    if use_sink:
        in_specs.append(_SMEM_SPEC)
        args.append(sink)
    in_specs.append(_ANY_SPEC)
    args.append(y)
    return pl.pallas_call(
        kern,
        grid=(BATCH,),
        in_specs=in_specs,
        out_specs=pl.BlockSpec((tq, 4 * LANES), lambda b: (CTX_BLK0 + b, 0)),
        out_shape=jax.ShapeDtypeStruct((N_ROWS, 4 * LANES), BF16),
        input_output_aliases={len(args) - 1: 0},
        compiler_params=_attn_params(1),
        name=name,
    )(*args)


def _context_split(y, q, k, v):
    tq = CTX_LEN
    kern = functools.partial(_dense_split_kernel, tq=tq, n_seg=1)
    return pl.pallas_call(
        kern,
        grid=(BATCH,),
        in_specs=[
            pl.BlockSpec((tq, 8 * LANES), lambda b: (CTX_BLK0 + b, 0)),
            pl.BlockSpec((CTX_LEN, 8 * LANES), lambda b: (CTX_BLK0 + b, 0)),
            pl.BlockSpec((CTX_LEN, 4 * LANES), lambda b: (CTX_BLK0 + b, 0)),
            _ANY_SPEC,
        ],
        out_specs=pl.BlockSpec((tq, 4 * LANES), lambda b: (CTX_BLK0 + b, 0)),
        out_shape=jax.ShapeDtypeStruct((N_ROWS, 4 * LANES), BF16),
        input_output_aliases={3: 0},
        compiler_params=_attn_params(1),
        name="attn_c_context",
    )(q, k, v, y)


def _merge_kernel(x_ref, sh_ref, sc_ref, gt_ref, g_ref, wg_ref, ya_ref, yb_ref, yc_ref,
                  wa_ref, wb_ref, wc_ref, wo_ref, o_ref):
    x = x_ref[...]
    u = _adaln_bf16(x, g_ref[...], sh_ref[0], sc_ref[0])
    m = None
    for idx, (y_ref, w_ref) in enumerate(((ya_ref, wa_ref), (yb_ref, wb_ref), (yc_ref, wc_ref))):
        gate = _dot(u, wg_ref[:, idx * D_MODEL:(idx + 1) * D_MODEL])
        term = jax.nn.sigmoid(gate) * _dot(y_ref[...], w_ref[...])
        m = term if m is None else m + term
    o_ref[...] = x + gt_ref[0] * _dot(m.astype(BF16), wo_ref[...])


def _merge(x, mods, layer, gain, wg, ya, yb, yc, wa, wb, wc, wo, n_rows):
    tm = TM_MERGE

    def lmat(r, c):
        return pl.BlockSpec((None, r, c), lambda i: (layer, 0, 0))

    y_spec = pl.BlockSpec((tm, 4 * LANES), lambda i: (i, 0))
    return pl.pallas_call(
        _merge_kernel,
        grid=(n_rows // tm,),
        in_specs=[
            pl.BlockSpec((tm, D_MODEL), lambda i: (i, 0)),
            _mod_spec(layer, 3, tm), _mod_spec(layer, 4, tm), _mod_spec(layer, 5, tm),
            pl.BlockSpec((None, 1, D_MODEL), lambda i: (layer, 0, 0)),
            lmat(D_MODEL, GATE_COLS),
            y_spec, y_spec, y_spec,
            lmat(4 * LANES, D_MODEL), lmat(4 * LANES, D_MODEL), lmat(4 * LANES, D_MODEL),
            lmat(D_MODEL, D_MODEL),
        ],
        out_specs=pl.BlockSpec((tm, D_MODEL), lambda i: (i, 0)),
        out_shape=jax.ShapeDtypeStruct((n_rows, D_MODEL), F32),
        compiler_params=pltpu.CompilerParams(
            dimension_semantics=("parallel",), vmem_limit_bytes=VMEM_LIMIT),
        name="merge",
    )(x, mods, mods, mods, gain, wg, ya, yb, yc, wa, wb, wc, wo)


def kernel(x, c, ctx, c_ctx, ada_w, ada_b, ffn1_norm, ffn1_w_in, ffn1_w_out, mix_norm, mix_w_in, a_q_norm, a_k_norm, a_sink, b_q_norm, b_k_norm, c_q_lat_norm, c_w_uq, c_kv_lat_norm, c_w_ukv, c_q_norm, c_k_norm, w_branch_a, w_branch_b, w_branch_c, mix_w_out, ffn2_norm, ffn2_w_in, ffn2_w_out):
    n_l = mix_w_in.shape[0]
    wp = _proj_weight(mix_w_in)
    wg = mix_w_in[:, :, KV_COLS + Q_COLS:].astype(BF16)
    ukv = c_w_ukv.reshape(n_l, C_KV_RANK, N_HEADS, C_NOPE_DIM + C_V_DIM)
    wkc = _c_head_blocks(ukv[..., :C_NOPE_DIM], None).astype(BF16)
    wvc = ukv[..., C_NOPE_DIM:].reshape(n_l, C_KV_RANK, N_HEADS * C_V_DIM).astype(BF16)
    uq = c_w_uq.reshape(n_l, C_Q_RANK, N_HEADS, C_QK_DIM)
    wqc = _c_head_blocks(uq[..., :C_NOPE_DIM], uq[..., C_NOPE_DIM:]).astype(BF16)

    def pair_rows(w):
        return w.reshape(n_l, 2, 4, HEAD_DIM, D_MODEL).transpose(0, 2, 1, 3, 4).reshape(n_l, 8 * HEAD_DIM, D_MODEL)

    wa = pair_rows(w_branch_a).astype(BF16)
    wb = pair_rows(w_branch_b).astype(BF16)
    wc = w_branch_c.astype(BF16)
    wo = mix_w_out.astype(BF16)
    f1_in, f1_out = ffn1_w_in.astype(BF16), ffn1_w_out.astype(BF16)
    f2_in, f2_out = ffn2_w_in.astype(BF16), ffn2_w_out.astype(BF16)
    gains = (_pair_gain(a_k_norm), _pair_gain(a_q_norm), _pair_gain(b_k_norm), _pair_gain(b_q_norm),
             c_kv_lat_norm[:, None, :], c_q_lat_norm[:, None, :], _c_gain(c_k_norm), _c_gain(c_q_norm))
    tables = _rope_tables(TM_PROJ)
    ffn1_g, mix_g, ffn2_g = ffn1_norm[:, None, :], mix_norm[:, None, :], ffn2_norm[:, None, :]

    cond = jnp.concatenate([c, c_ctx[None, :], jnp.zeros((GROUP_PAD - N_GROUPS, D_MODEL), F32)], axis=0)
    mods = _modulation(cond, ada_w, ada_b).reshape(DEPTH * N_MOD * GROUP_PAD, 1, D_MODEL)

    xs = None
    for l in range(DEPTH):
        last = l == DEPTH - 1
        if l == 0:
            xs = _ffn(x.reshape(N_LAT, D_MODEL), mods, l, 0, ffn1_g, f1_in, f1_out, N_ROWS)
            xs = _ffn(ctx.reshape(N_CTX, D_MODEL), mods, l, 0, ffn1_g, f1_in, f1_out, N_ROWS, N_LAT, xs)
        else:
            xs = _ffn(xs, mods, l, 0, ffn1_g, f1_in, f1_out, N_ROWS)
        kvab, qa, qb, qc, kc, vc = _project(xs, mods, l, mix_g, wp, wkc, wvc, wqc, gains, tables)
        n_rows = N_LAT if last else N_ROWS
        ya = _latent_window(qa, kvab, a_sink, l, n_rows)
        yb = _latent_dense_shared(qb, kvab, 2, 3, l, "attn_b_latent", n_rows)
        yc = _latent_dense_split(qc, kc, vc, n_rows)
        if not last:
            ya = _context_shared(ya, qa, kvab, 0, 1, a_sink, l, "attn_a_context")
            yb = _context_shared(yb, qb, kvab, 2, 3, None, l, "attn_b_context")
            yc = _context_split(yc, qc, kc, vc)
        xs = _merge(xs, mods, l, mix_g, wg, ya, yb, yc, wa, wb, wc, wo, n_rows)
        xs = _ffn(xs, mods, l, 6, ffn2_g, f2_in, f2_out, n_rows)
    return xs.reshape(BATCH, SEQ, D_MODEL)
```

```python
import functools
import math

import numpy as np
import jax
import jax.numpy as jnp
from jax import lax
from jax.experimental import pallas as pl
from jax.experimental.pallas import tpu as pltpu

F32 = jnp.float32
BF16 = jnp.bfloat16

D_MODEL = 1024
BATCH = 8
SEQ = 2048
DEPTH = 2
GRID_W = 64
CTX_LEN = 256
HEAD_DIM = 64
WINDOW = 128
ROPE_THETA = 10000.0
NORM_EPS = 1e-6
N_MOD = 9
D_FF = 2816
N_HEADS = 8
C_NOPE_DIM = 64
C_ROPE_DIM = 32
C_QK_DIM = C_NOPE_DIM + C_ROPE_DIM
C_V_DIM = 64
C_Q_RANK = 384
C_KV_RANK = 256
KV_COLS = 800
Q_COLS = 1408
GATE_COLS = 3 * D_MODEL

LANES = 128
N_LAT = BATCH * SEQ
N_CTX = BATCH * CTX_LEN
N_ROWS = N_LAT + N_CTX
N_GROUPS = BATCH + 1
GROUP_PAD = 16
CTX_BLK0 = N_LAT // CTX_LEN

VMEM_LIMIT = 56 * 1024 * 1024
LOG2E = math.log2(math.e)

PB_AK, PB_AV, PB_BK, PB_BV, PB_CKV, PB_AQ, PB_BQ, PB_CKR, PB_CQ, PB_END = 0, 1, 2, 3, 4, 6, 10, 14, 15, 18
PROJ_COLS = PB_END * LANES

TM_FFN = 512
TM_PROJ = 512
TM_MERGE = 512
TQ_DENSE_SHARED = 512
TQ_DENSE_SPLIT = 512
TQ_WINDOW = 256

_SMEM_SPEC = pl.BlockSpec(memory_space=pltpu.SMEM)
_ANY_SPEC = pl.BlockSpec(memory_space=pl.ANY)


def _proj_weight_natural(w_in):
    def zeros(n):
        return jnp.zeros(w_in.shape[:-1] + (n,), w_in.dtype)

    q0 = KV_COLS
    blocks = [w_in[..., 0:768]]
    for base in (q0, q0 + 512):
        for j in range(4):
            blocks += [w_in[..., base + j * 64:base + (j + 1) * 64],
                       w_in[..., base + (j + 4) * 64:base + (j + 5) * 64]]
    blocks += [zeros(C_NOPE_DIM), w_in[..., 768:800], zeros(32)]
    blocks.append(w_in[..., q0 + 1024:q0 + 1024 + C_Q_RANK])
    return jnp.concatenate(blocks, axis=-1).astype(BF16)


PROJ_BLOCK_KINDS = (("pair", "id") * 2 + ("id",) * 2 + ("pair",) * 8 + ("c",) + ("id",) * 3)


def _pad_heads(w, n_used):
    lead = w.shape[:-1]
    blk = jnp.concatenate([w[..., :n_used], jnp.zeros(lead + (LANES - n_used,), w.dtype)], axis=-1)
    return blk.reshape(lead[:-1] + (N_HEADS * LANES,)).astype(BF16)


def _perm_matrix(kind):
    r = lax.broadcasted_iota(jnp.int32, (LANES, LANES), 0)
    c = lax.broadcasted_iota(jnp.int32, (LANES, LANES), 1)
    if kind == "pair":
        src = ((c >> 5) & 1) * 64 + 2 * (c & 31) + (c >> 6)
    else:
        src = jnp.where(c < 16, 64 + 2 * c,
                        jnp.where(c < 64, c - 16,
                                  jnp.where(c < 80, 2 * c - 63,
                                            jnp.where(c < 96, c - 32, -1))))
    return jnp.where(r == src, 1.0, 0.0).astype(BF16)


def _permute_kernel(w_ref, o_ref, *, kinds):
    mats = {k: _perm_matrix(k) for k in set(kinds) if k != "id"}
    for j, kind in enumerate(kinds):
        sl = slice(j * LANES, (j + 1) * LANES)
        if kind == "id":
            o_ref[:, sl] = w_ref[:, sl]
        else:
            o_ref[:, sl] = _dot(w_ref[:, sl], mats[kind]).astype(BF16)


def _permute_cols(w, kinds):
    n_l, r, c = w.shape
    spec = pl.BlockSpec((None, r, c), lambda l: (l, 0, 0))
    return pl.pallas_call(
        functools.partial(_permute_kernel, kinds=kinds),
        grid=(n_l,),
        in_specs=[spec],
        out_specs=spec,
        out_shape=jax.ShapeDtypeStruct(w.shape, BF16),
        compiler_params=pltpu.CompilerParams(dimension_semantics=("parallel",), vmem_limit_bytes=VMEM_LIMIT),
        name="permute_cols",
    )(w)


def _pair_gain(g):
    e, o = g[:, 0::2], g[:, 1::2]
    return jnp.stack([jnp.concatenate([e, e, o, o], axis=-1), jnp.concatenate([o, o, e, e], axis=-1)], axis=1)


def _c_gain(g):
    nope, rope = g[:, :C_NOPE_DIM], g[:, C_NOPE_DIM:]
    zeros = jnp.zeros((g.shape[0], 32), g.dtype)
    lo = jnp.concatenate([rope[:, 0::2], nope[:, :48]], axis=-1)
    hi = jnp.concatenate([rope[:, 1::2], nope[:, 48:], zeros], axis=-1)
    return jnp.stack([jnp.concatenate([lo, hi], axis=-1), jnp.concatenate([hi, lo], axis=-1)], axis=1)


def _rope_tables(tm):
    pos = np.arange(SEQ)
    row = (pos // GRID_W).astype(np.float32)
    col = (pos % GRID_W).astype(np.float32)

    def angles(rot_dim):
        n_freq = rot_dim // 4
        inv = jnp.asarray(ROPE_THETA, F32) ** (-jnp.arange(n_freq, dtype=F32) / n_freq)
        return jnp.concatenate([jnp.asarray(row)[:, None] * inv, jnp.asarray(col)[:, None] * inv], axis=-1)

    a64 = angles(HEAD_DIM)
    c, s = jnp.cos(a64), jnp.sin(a64)
    cos_ab = jnp.concatenate([c, c, c, c], axis=-1)
    sin_ab = jnp.concatenate([-s, -s, s, s], axis=-1)
    a32 = angles(C_ROPE_DIM)
    c, s = jnp.cos(a32), jnp.sin(a32)
    one48, zero48 = jnp.ones((SEQ, 48), F32), jnp.zeros((SEQ, 48), F32)
    cos_c = jnp.concatenate([c, one48, c, one48], axis=-1)
    sin_c = jnp.concatenate([-s, zero48, s, zero48], axis=-1)
    ident_c, ident_s = jnp.ones((tm, LANES), F32), jnp.zeros((tm, LANES), F32)
    return (jnp.concatenate([cos_ab, ident_c]), jnp.concatenate([sin_ab, ident_s]),
            jnp.concatenate([cos_c, ident_c]), jnp.concatenate([sin_c, ident_s]))


def _adaln_bf16(x, gain, shift, scale):
    inv = lax.rsqrt(jnp.mean(x * x, axis=-1, keepdims=True) + NORM_EPS)
    return ((x * inv) * (gain * (1.0 + scale)) + shift).astype(BF16)


def _rms(x, gain, n):
    return x * lax.rsqrt(jnp.sum(x * x, axis=-1, keepdims=True) * (1.0 / n) + NORM_EPS) * gain


def _rope_gain_tables(g_ref, cos, sin, scale=1.0):
    return cos * (g_ref[0:1, :] * scale), sin * (g_ref[1:2, :] * scale)


def _norm_rotate(x, inv, gcos, gsin):
    return ((x * gcos + pltpu.roll(x, 64, axis=1) * gsin) * inv).astype(BF16)


def _group_ones(width, group_of):
    r = lax.broadcasted_iota(jnp.int32, (width, width), 0)
    c = lax.broadcasted_iota(jnp.int32, (width, width), 1)
    return jnp.where(group_of(r) == group_of(c), 1.0, 0.0).astype(BF16)


def _inv_rms_mxu(x, ones_mat, n):
    x2 = x * x
    hi = x2.astype(BF16)
    lo = (x2 - hi.astype(F32)).astype(BF16)
    return lax.rsqrt((_dot(hi, ones_mat) + _dot(lo, ones_mat)) * (1.0 / n) + NORM_EPS)


def _dot(a, b):
    return jnp.dot(a, b, preferred_element_type=F32)


def _dot_nt(a, b):
    return lax.dot_general(a, b, (((1,), (1,)), ((), ())), preferred_element_type=F32)


def _mod_kernel(cond_ref, w_ref, b_ref, o_ref):
    c = cond_ref[...]
    s = (c * jax.nn.sigmoid(c)).astype(BF16)
    o_ref[...] = _dot(s, w_ref[...].astype(BF16)) + b_ref[...]


def _modulation(cond, ada_w, ada_b):
    return pl.pallas_call(
        _mod_kernel,
        grid=(DEPTH, N_MOD),
        in_specs=[
            pl.BlockSpec((GROUP_PAD, D_MODEL), lambda l, k: (0, 0)),
            pl.BlockSpec((None, D_MODEL, D_MODEL), lambda l, k: (l, 0, k)),
            pl.BlockSpec((None, 1, D_MODEL), lambda l, k: (l, 0, k)),
        ],
        out_specs=pl.BlockSpec((None, None, GROUP_PAD, D_MODEL), lambda l, k: (l, k, 0, 0)),
        out_shape=jax.ShapeDtypeStruct((DEPTH, N_MOD, GROUP_PAD, D_MODEL), F32),
        compiler_params=pltpu.CompilerParams(
            dimension_semantics=("arbitrary", "arbitrary"), vmem_limit_bytes=VMEM_LIMIT),
        name="modulation",
    )(cond, ada_w, ada_b.reshape(DEPTH, 1, N_MOD * D_MODEL))


def _mod_spec(layer, k, tm, row0=0):
    base = (layer * N_MOD + k) * GROUP_PAD
    return pl.BlockSpec((1, 1, D_MODEL), lambda i, *_: (base + (row0 + i * tm) // SEQ, 0, 0))


def _ffn_kernel(x_ref, sh_ref, sc_ref, gt_ref, g_ref, wi_ref, wo_ref, *rest):
    o_ref = rest[-1]
    x = x_ref[...]
    h = _adaln_bf16(x, g_ref[...], sh_ref[0], sc_ref[0])
    gate = _dot(h, wi_ref[:, :D_FF])
    up = _dot(h, wi_ref[:, D_FF:])
    act = (gate * jax.nn.sigmoid(gate) * up).astype(BF16)
    o_ref[...] = x + 0.5 * gt_ref[0] * _dot(act, wo_ref[...])


def _ffn(x, mods, layer, k0, gain, w_in, w_out, n_out, row0=0, out_buf=None):
    tm = TM_FFN
    blk0 = row0 // tm
    resident = pl.Buffered(1)
    in_specs = [
        pl.BlockSpec((tm, D_MODEL), lambda i: (i, 0)),
        _mod_spec(layer, k0, tm, row0), _mod_spec(layer, k0 + 1, tm, row0), _mod_spec(layer, k0 + 2, tm, row0),
        pl.BlockSpec((None, 1, D_MODEL), lambda i: (layer, 0, 0)),
        pl.BlockSpec((None, D_MODEL, 2 * D_FF), lambda i: (layer, 0, 0), pipeline_mode=resident),
        pl.BlockSpec((None, D_FF, D_MODEL), lambda i: (layer, 0, 0), pipeline_mode=resident),
    ]
    args = [x, mods, mods, mods, gain, w_in, w_out]
    aliases = {}
    if out_buf is not None:
        in_specs.append(_ANY_SPEC)
        args.append(out_buf)
        aliases = {len(args) - 1: 0}
    return pl.pallas_call(
        _ffn_kernel,
        grid=(x.shape[0] // tm,),
        in_specs=in_specs,
        out_specs=pl.BlockSpec((tm, D_MODEL), lambda i: (blk0 + i, 0)),
        out_shape=jax.ShapeDtypeStruct((n_out, D_MODEL), F32),
        input_output_aliases=aliases,
        compiler_params=pltpu.CompilerParams(
            dimension_semantics=("parallel",), vmem_limit_bytes=VMEM_LIMIT),
        name="ffn",
    )(*args)


def _proj_kernel(x_ref, sh_ref, sc_ref, g_ref, wp_ref, wkc_ref, wvc_ref, wqc_ref,
                 gak_ref, gaq_ref, gbk_ref, gbq_ref, gkvl_ref, gql_ref, gck_ref, gcq_ref,
                 cab_ref, sab_ref, cc_ref, sc_c_ref,
                 kvab_ref, qa_ref, qb_ref, qc_ref, kc_ref, vc_ref):
    h = _adaln_bf16(x_ref[...], g_ref[...], sh_ref[0], sc_ref[0])

    def project(b, n):
        return _dot(h, wp_ref[:, b * LANES:(b + n) * LANES])

    cab, sab = cab_ref[...], sab_ref[...]
    qk_scale = HEAD_DIM ** -0.5 * LOG2E
    pair_ones = _group_ones(LANES, lambda l: (l >> 5) & 1)
    pair_ones2 = _group_ones(2 * LANES, lambda l: ((l >> 5) & 1) + 2 * (l >> 7))
    head_ones2 = _group_ones(2 * LANES, lambda l: l >> 7)

    for b, g_ref_k in ((PB_AK, gak_ref), (PB_BK, gbk_ref)):
        r = project(b, 2)
        gcos, gsin = _rope_gain_tables(g_ref_k, cab, sab)
        k = r[:, :LANES]
        kvab_ref[:, b * LANES:(b + 1) * LANES] = _norm_rotate(k, _inv_rms_mxu(k, pair_ones, HEAD_DIM), gcos, gsin)
        kvab_ref[:, (b + 1) * LANES:(b + 2) * LANES] = r[:, LANES:].astype(BF16)

    def store_two(o_ref, blk, x2blk, inv, gcos, gsin):
        for t in range(2):
            sl = slice(t * LANES, (t + 1) * LANES)
            o_ref[:, (blk + t) * LANES:(blk + t + 1) * LANES] = _norm_rotate(x2blk[:, sl], inv[:, sl], gcos, gsin)

    for b, g_ref_q, q_ref in ((PB_AQ, gaq_ref, qa_ref), (PB_BQ, gbq_ref, qb_ref)):
        gcos, gsin = _rope_gain_tables(g_ref_q, cab, sab, qk_scale)
        for j in (0, 2):
            r = project(b + j, 2)
            store_two(q_ref, j, r, _inv_rms_mxu(r, pair_ones2, HEAD_DIM), gcos, gsin)

    cc, sc_c = cc_ref[...], sc_c_ref[...]
    ckv = _rms(project(PB_CKV, 2), gkvl_ref[...], C_KV_RANK).astype(BF16)
    vc_ref[...] = _dot(ckv, wvc_ref[...]).astype(BF16)
    r = project(PB_CKR, 4)
    k_rope2 = jnp.concatenate([r[:, :LANES], r[:, :LANES]], axis=1)
    cq = _rms(r[:, LANES:], gql_ref[...], C_Q_RANK).astype(BF16)
    gcos_k, gsin_k = _rope_gain_tables(gck_ref, cc, sc_c)
    gcos_q, gsin_q = _rope_gain_tables(gcq_ref, cc, sc_c, C_QK_DIM ** -0.5 * LOG2E)
    for hp in range(N_HEADS // 2):
        sl2 = slice(2 * hp * LANES, (2 * hp + 2) * LANES)
        k2 = _dot(ckv, wkc_ref[:, sl2]) + k_rope2
        store_two(kc_ref, 2 * hp, k2, _inv_rms_mxu(k2, head_ones2, C_QK_DIM), gcos_k, gsin_k)
        q2 = _dot(cq, wqc_ref[:, sl2])
        store_two(qc_ref, 2 * hp, q2, _inv_rms_mxu(q2, head_ones2, C_QK_DIM), gcos_q, gsin_q)


def _project(x, mods, layer, gain, wp, wkc, wvc, wqc, gains, tables):
    tm = TM_PROJ
    n_lat_tiles = N_LAT // tm
    tiles_per_seq = SEQ // tm

    def tab_idx(i):
        return (jnp.where(i < n_lat_tiles, i % tiles_per_seq, tiles_per_seq), 0)

    def lvec(n):
        return pl.BlockSpec((None, 1, n), lambda i: (layer, 0, 0))

    def lmat(r, c):
        return pl.BlockSpec((None, r, c), lambda i: (layer, 0, 0))

    def rows(c, dtype=BF16):
        return pl.BlockSpec((tm, c), lambda i: (i, 0)), jax.ShapeDtypeStruct((N_ROWS, c), dtype)

    out = [rows(4 * LANES), rows(4 * LANES), rows(4 * LANES), rows(8 * LANES), rows(8 * LANES), rows(4 * LANES)]
    tab_spec = pl.BlockSpec((tm, LANES), tab_idx)
    return pl.pallas_call(
        _proj_kernel,
        grid=(N_ROWS // tm,),
        in_specs=[
            pl.BlockSpec((tm, D_MODEL), lambda i: (i, 0)),
            _mod_spec(layer, 3, tm), _mod_spec(layer, 4, tm),
            lvec(D_MODEL),
            lmat(D_MODEL, PROJ_COLS), lmat(C_KV_RANK, 8 * LANES), lmat(C_KV_RANK, 4 * LANES),
            lmat(C_Q_RANK, 8 * LANES),
            lmat(2, LANES), lmat(2, LANES), lmat(2, LANES), lmat(2, LANES), lvec(C_KV_RANK), lvec(C_Q_RANK),
            lmat(2, LANES), lmat(2, LANES),
            tab_spec, tab_spec, tab_spec, tab_spec,
        ],
        out_specs=[o[0] for o in out],
        out_shape=[o[1] for o in out],
        compiler_params=pltpu.CompilerParams(
            dimension_semantics=("parallel",), vmem_limit_bytes=VMEM_LIMIT),
        name="project",
    )(x, mods, mods, gain, wp, wkc, wvc, wqc, *gains, *tables)


def _with_ones(v):
    return jnp.concatenate([v, jnp.ones_like(v)], axis=1)


def _attend(q, segs, sink_col):
    scores = []
    for k, _, bias in segs:
        s = _dot_nt(q, k)
        scores.append(s if bias is None else s + bias)
    m = functools.reduce(jnp.maximum, [jnp.max(s, axis=-1, keepdims=True) for s in scores])
    if sink_col is not None:
        m = jnp.maximum(m, sink_col)
    acc = None
    for s, (_, v, _) in zip(scores, segs):
        part = _dot(jnp.exp2(s - m).astype(BF16), v)
        acc = part if acc is None else acc + part
    denom = acc[:, LANES:]
    if sink_col is not None:
        denom = denom + jnp.exp2(sink_col - m)
    return acc[:, :LANES] / denom


def _pair_masks():
    lane = lax.broadcasted_iota(jnp.int32, (1, LANES), 1)
    return (lane & 63) < 32, lane < 64


def _dense_shared_kernel(*refs, layer, tq, n_seg, use_sink):
    q_ref = refs[0]
    kv = refs[1:1 + 2 * n_seg]
    sink_ref = refs[1 + 2 * n_seg] if use_sink else None
    o_ref = refs[-1]
    head0, low = _pair_masks()
    zero = jnp.zeros((), BF16)
    segs = [(kv[2 * s][...], _with_ones(kv[2 * s + 1][...]), None) for s in range(n_seg)]
    for j in range(4):
        qb = q_ref[:, j * LANES:(j + 1) * LANES]
        outs = []
        for half in range(2):
            q = jnp.where(head0, qb, zero) if half == 0 else jnp.where(head0, zero, qb)
            sink_col = (jnp.full((tq, 1), sink_ref[layer, j + 4 * half] * LOG2E, F32) if use_sink else None)
            outs.append(_attend(q, segs, sink_col))
        o_ref[:, j * LANES:(j + 1) * LANES] = jnp.where(low, outs[0], outs[1]).astype(o_ref.dtype)


def _dense_split_kernel(*refs, tq, n_seg):
    q_ref = refs[0]
    kv = refs[1:1 + 2 * n_seg]
    o_ref = refs[-1]
    _, low = _pair_masks()
    for j in range(4):
        vsl = slice(j * LANES, (j + 1) * LANES)
        vals = [_with_ones(kv[2 * s + 1][:, vsl]) for s in range(n_seg)]
        outs = []
        for half in range(2):
            ksl = slice((2 * j + half) * LANES, (2 * j + half + 1) * LANES)
            segs = [(kv[2 * s][:, ksl], vals[s], None) for s in range(n_seg)]
            outs.append(_attend(q_ref[:, ksl], segs, None))
        o_ref[:, vsl] = jnp.where(low, outs[0], outs[1]).astype(o_ref.dtype)


def _window_kernel(q_ref, kl_ref, vl_ref, kc_ref, vc_ref, sink_ref, o_ref, *, layer, tq):
    span = tq + 2 * WINDOW
    start = pl.program_id(1) * tq
    lo = pl.multiple_of(jnp.clip(start - WINDOW, 0, SEQ - span), WINDOW)
    head0, low = _pair_masks()
    zero = jnp.zeros((), BF16)
    q_pos = start + lax.broadcasted_iota(jnp.int32, (tq, span), 0)
    k_pos = lo + lax.broadcasted_iota(jnp.int32, (tq, span), 1)
    band = jnp.where(jnp.abs(q_pos - k_pos) <= WINDOW, 0.0, -jnp.inf)
    band = jnp.concatenate([band, band], axis=0)
    segs = [(kl_ref[pl.ds(lo, span), :], _with_ones(vl_ref[pl.ds(lo, span), :]), band),
            (kc_ref[...], _with_ones(vc_ref[...]), None)]
    for j in range(4):
        qb = q_ref[:, j * LANES:(j + 1) * LANES]
        q = jnp.concatenate([jnp.where(head0, qb, zero), jnp.where(head0, zero, qb)], axis=0)
        sink_col = jnp.concatenate([jnp.full((tq, 1), sink_ref[layer, j] * LOG2E, F32),
                                    jnp.full((tq, 1), sink_ref[layer, j + 4] * LOG2E, F32)], axis=0)
        o = _attend(q, segs, sink_col)
        o_ref[:, j * LANES:(j + 1) * LANES] = jnp.where(low, o[:tq], o[tq:]).astype(o_ref.dtype)


def _attn_params(n_grid):
    return pltpu.CompilerParams(dimension_semantics=("parallel",) * n_grid, vmem_limit_bytes=VMEM_LIMIT)


def _latent_dense_shared(q, kvab, k_blk, v_blk, layer, name, n_out):
    tq = TQ_DENSE_SHARED
    nq = SEQ // tq
    kern = functools.partial(_dense_shared_kernel, layer=layer, tq=tq, n_seg=2, use_sink=False)
    return pl.pallas_call(
        kern,
        grid=(BATCH, nq),
        in_specs=[
            pl.BlockSpec((tq, 4 * LANES), lambda b, i: (b * nq + i, 0)),
            pl.BlockSpec((SEQ, LANES), lambda b, i: (b, k_blk)),
            pl.BlockSpec((SEQ, LANES), lambda b, i: (b, v_blk)),
            pl.BlockSpec((CTX_LEN, LANES), lambda b, i: (CTX_BLK0 + b, k_blk)),
            pl.BlockSpec((CTX_LEN, LANES), lambda b, i: (CTX_BLK0 + b, v_blk)),
        ],
        out_specs=pl.BlockSpec((tq, 4 * LANES), lambda b, i: (b * nq + i, 0)),
        out_shape=jax.ShapeDtypeStruct((n_out, 4 * LANES), BF16),
        compiler_params=_attn_params(2),
        name=name,
    )(q, kvab, kvab, kvab, kvab)


def _latent_window(q, kvab, sink, layer, n_out):
    tq = TQ_WINDOW
    nq = SEQ // tq
    kern = functools.partial(_window_kernel, layer=layer, tq=tq)
    return pl.pallas_call(
        kern,
        grid=(BATCH, nq),
        in_specs=[
            pl.BlockSpec((tq, 4 * LANES), lambda b, i: (b * nq + i, 0)),
            pl.BlockSpec((SEQ, LANES), lambda b, i: (b, 0)),
            pl.BlockSpec((SEQ, LANES), lambda b, i: (b, 1)),
            pl.BlockSpec((CTX_LEN, LANES), lambda b, i: (CTX_BLK0 + b, 0)),
            pl.BlockSpec((CTX_LEN, LANES), lambda b, i: (CTX_BLK0 + b, 1)),
            _SMEM_SPEC,
        ],
        out_specs=pl.BlockSpec((tq, 4 * LANES), lambda b, i: (b * nq + i, 0)),
        out_shape=jax.ShapeDtypeStruct((n_out, 4 * LANES), BF16),
        compiler_params=_attn_params(2),
        name="attn_a_window",
    )(q, kvab, kvab, kvab, kvab, sink)


def _latent_dense_split(q, k, v, n_out):
    tq = TQ_DENSE_SPLIT
    nq = SEQ // tq
    kern = functools.partial(_dense_split_kernel, tq=tq, n_seg=2)
    return pl.pallas_call(
        kern,
        grid=(BATCH, nq),
        in_specs=[
            pl.BlockSpec((tq, 8 * LANES), lambda b, i: (b * nq + i, 0)),
            pl.BlockSpec((SEQ, 8 * LANES), lambda b, i: (b, 0)),
            pl.BlockSpec((SEQ, 4 * LANES), lambda b, i: (b, 0)),
            pl.BlockSpec((CTX_LEN, 8 * LANES), lambda b, i: (CTX_BLK0 + b, 0)),
            pl.BlockSpec((CTX_LEN, 4 * LANES), lambda b, i: (CTX_BLK0 + b, 0)),
        ],
        out_specs=pl.BlockSpec((tq, 4 * LANES), lambda b, i: (b * nq + i, 0)),
        out_shape=jax.ShapeDtypeStruct((n_out, 4 * LANES), BF16),
        compiler_params=_attn_params(2),
        name="attn_c_latent",
    )(q, k, v, k, v)


def _context_shared(y, q, kvab, k_blk, v_blk, sink, layer, name):
    tq = CTX_LEN
    use_sink = sink is not None
    kern = functools.partial(_dense_shared_kernel, layer=layer, tq=tq, n_seg=1, use_sink=use_sink)
    in_specs = [
        pl.BlockSpec((tq, 4 * LANES), lambda b: (CTX_BLK0 + b, 0)),
        pl.BlockSpec((CTX_LEN, LANES), lambda b: (CTX_BLK0 + b, k_blk)),
        pl.BlockSpec((CTX_LEN, LANES), lambda b: (CTX_BLK0 + b, v_blk)),
    ]
    args = [q, kvab, kvab]
    if use_sink:
        in_specs.append(_SMEM_SPEC)
        args.append(sink)
    in_specs.append(_ANY_SPEC)
    args.append(y)
    return pl.pallas_call(
        kern,
        grid=(BATCH,),
        in_specs=in_specs,
        out_specs=pl.BlockSpec((tq, 4 * LANES), lambda b: (CTX_BLK0 + b, 0)),
        out_shape=jax.ShapeDtypeStruct((N_ROWS, 4 * LANES), BF16),
        input_output_aliases={len(args) - 1: 0},
        compiler_params=_attn_params(1),
        name=name,
    )(*args)


def _context_split(y, q, k, v):
    tq = CTX_LEN
    kern = functools.partial(_dense_split_kernel, tq=tq, n_seg=1)
    return pl.pallas_call(
        kern,
        grid=(BATCH,),
        in_specs=[
            pl.BlockSpec((tq, 8 * LANES), lambda b: (CTX_BLK0 + b, 0)),
            pl.BlockSpec((CTX_LEN, 8 * LANES), lambda b: (CTX_BLK0 + b, 0)),
            pl.BlockSpec((CTX_LEN, 4 * LANES), lambda b: (CTX_BLK0 + b, 0)),
            _ANY_SPEC,
        ],
        out_specs=pl.BlockSpec((tq, 4 * LANES), lambda b: (CTX_BLK0 + b, 0)),
        out_shape=jax.ShapeDtypeStruct((N_ROWS, 4 * LANES), BF16),
        input_output_aliases={3: 0},
        compiler_params=_attn_params(1),
        name="attn_c_context",
    )(q, k, v, y)


def _merge_kernel(x_ref, sh_ref, sc_ref, gt_ref, g_ref, wg_ref, ya_ref, yb_ref, yc_ref,
                  wa_ref, wb_ref, wc_ref, wo_ref, o_ref):
    x = x_ref[...]
    u = _adaln_bf16(x, g_ref[...], sh_ref[0], sc_ref[0])
    m = None
    for idx, (y_ref, w_ref) in enumerate(((ya_ref, wa_ref), (yb_ref, wb_ref), (yc_ref, wc_ref))):
        gate = _dot(u, wg_ref[:, idx * D_MODEL:(idx + 1) * D_MODEL])
        term = jax.nn.sigmoid(gate) * _dot(y_ref[...], w_ref[...])
        m = term if m is None else m + term
    o_ref[...] = x + gt_ref[0] * _dot(m.astype(BF16), wo_ref[...])


def _merge(x, mods, layer, gain, wg, ya, yb, yc, wa, wb, wc, wo, n_rows):
    tm = TM_MERGE

    def lmat(r, c):
        return pl.BlockSpec((None, r, c), lambda i: (layer, 0, 0))

    y_spec = pl.BlockSpec((tm, 4 * LANES), lambda i: (i, 0))
    return pl.pallas_call(
        _merge_kernel,
        grid=(n_rows // tm,),
        in_specs=[
            pl.BlockSpec((tm, D_MODEL), lambda i: (i, 0)),
            _mod_spec(layer, 3, tm), _mod_spec(layer, 4, tm), _mod_spec(layer, 5, tm),
            pl.BlockSpec((None, 1, D_MODEL), lambda i: (layer, 0, 0)),
            lmat(D_MODEL, GATE_COLS),
            y_spec, y_spec, y_spec,
            lmat(4 * LANES, D_MODEL), lmat(4 * LANES, D_MODEL), lmat(4 * LANES, D_MODEL),
            lmat(D_MODEL, D_MODEL),
        ],
        out_specs=pl.BlockSpec((tm, D_MODEL), lambda i: (i, 0)),
        out_shape=jax.ShapeDtypeStruct((n_rows, D_MODEL), F32),
        compiler_params=pltpu.CompilerParams(
            dimension_semantics=("parallel",), vmem_limit_bytes=VMEM_LIMIT),
        name="merge",
    )(x, mods, mods, mods, gain, wg, ya, yb, yc, wa, wb, wc, wo)


def kernel(x, c, ctx, c_ctx, ada_w, ada_b, ffn1_norm, ffn1_w_in, ffn1_w_out, mix_norm, mix_w_in, a_q_norm, a_k_norm, a_sink, b_q_norm, b_k_norm, c_q_lat_norm, c_w_uq, c_kv_lat_norm, c_w_ukv, c_q_norm, c_k_norm, w_branch_a, w_branch_b, w_branch_c, mix_w_out, ffn2_norm, ffn2_w_in, ffn2_w_out):
    n_l = mix_w_in.shape[0]
    wp = _permute_cols(_proj_weight_natural(mix_w_in), PROJ_BLOCK_KINDS)
    wg = mix_w_in[:, :, KV_COLS + Q_COLS:].astype(BF16)
    ukv = c_w_ukv.reshape(n_l, C_KV_RANK, N_HEADS, C_NOPE_DIM + C_V_DIM)
    wkc = _permute_cols(_pad_heads(ukv, C_NOPE_DIM), ("c",) * N_HEADS)
    wvc = ukv[..., C_NOPE_DIM:].reshape(n_l, C_KV_RANK, N_HEADS * C_V_DIM).astype(BF16)
    uq = c_w_uq.reshape(n_l, C_Q_RANK, N_HEADS, C_QK_DIM)
    wqc = _permute_cols(_pad_heads(uq, C_QK_DIM), ("c",) * N_HEADS)

    def pair_rows(w):
        return w.reshape(n_l, 2, 4, HEAD_DIM, D_MODEL).transpose(0, 2, 1, 3, 4).reshape(n_l, 8 * HEAD_DIM, D_MODEL)

    wa = pair_rows(w_branch_a).astype(BF16)
    wb = pair_rows(w_branch_b).astype(BF16)
    wc = w_branch_c.astype(BF16)
    wo = mix_w_out.astype(BF16)
    f1_in, f1_out = ffn1_w_in.astype(BF16), ffn1_w_out.astype(BF16)
    f2_in, f2_out = ffn2_w_in.astype(BF16), ffn2_w_out.astype(BF16)
    gains = (_pair_gain(a_k_norm), _pair_gain(a_q_norm), _pair_gain(b_k_norm), _pair_gain(b_q_norm),
             c_kv_lat_norm[:, None, :], c_q_lat_norm[:, None, :], _c_gain(c_k_norm), _c_gain(c_q_norm))
    tables = _rope_tables(TM_PROJ)
    ffn1_g, mix_g, ffn2_g = ffn1_norm[:, None, :], mix_norm[:, None, :], ffn2_norm[:, None, :]

    cond = jnp.concatenate([c, c_ctx[None, :], jnp.zeros((GROUP_PAD - N_GROUPS, D_MODEL), F32)], axis=0)
    mods = _modulation(cond, ada_w, ada_b).reshape(DEPTH * N_MOD * GROUP_PAD, 1, D_MODEL)

    xs = None
    for l in range(DEPTH):
        last = l == DEPTH - 1
        if l == 0:
            xs = _ffn(x.reshape(N_LAT, D_MODEL), mods, l, 0, ffn1_g, f1_in, f1_out, N_ROWS)
            xs = _ffn(ctx.reshape(N_CTX, D_MODEL), mods, l, 0, ffn1_g, f1_in, f1_out, N_ROWS, N_LAT, xs)
        else:
            xs = _ffn(xs, mods, l, 0, ffn1_g, f1_in, f1_out, N_ROWS)
        kvab, qa, qb, qc, kc, vc = _project(xs, mods, l, mix_g, wp, wkc, wvc, wqc, gains, tables)
        n_rows = N_LAT if last else N_ROWS
        ya = _latent_window(qa, kvab, a_sink, l, n_rows)
        yb = _latent_dense_shared(qb, kvab, 2, 3, l, "attn_b_latent", n_rows)
        yc = _latent_dense_split(qc, kc, vc, n_rows)
        if not last:
            ya = _context_shared(ya, qa, kvab, 0, 1, a_sink, l, "attn_a_context")
            yb = _context_shared(yb, qb, kvab, 2, 3, None, l, "attn_b_context")
            yc = _context_split(yc, qc, kc, vc)
        xs = _merge(xs, mods, l, mix_g, wg, ya, yb, yc, wa, wb, wc, wo, n_rows)
        xs = _ffn(xs, mods, l, 6, ffn2_g, f2_in, f2_out, n_rows)
    return xs.reshape(BATCH, SEQ, D_MODEL)
```

```python
import functools
import math

import numpy as np
import jax
import jax.numpy as jnp
from jax import lax
from jax.experimental import pallas as pl
from jax.experimental.pallas import tpu as pltpu

F32 = jnp.float32
BF16 = jnp.bfloat16

D_MODEL = 1024
BATCH = 8
SEQ = 2048
DEPTH = 2
GRID_W = 64
CTX_LEN = 256
HEAD_DIM = 64
WINDOW = 128
ROPE_THETA = 10000.0
NORM_EPS = 1e-6
N_MOD = 9
D_FF = 2816
N_HEADS = 8
C_NOPE_DIM = 64
C_ROPE_DIM = 32
C_QK_DIM = C_NOPE_DIM + C_ROPE_DIM
C_V_DIM = 64
C_Q_RANK = 384
C_KV_RANK = 256
KV_COLS = 800
Q_COLS = 1408
GATE_COLS = 3 * D_MODEL

LANES = 128
N_LAT = BATCH * SEQ
N_CTX = BATCH * CTX_LEN
N_ROWS = N_LAT + N_CTX
N_GROUPS = BATCH + 1
GROUP_PAD = 16
CTX_BLK0 = N_LAT // CTX_LEN

VMEM_LIMIT = 56 * 1024 * 1024
LOG2E = math.log2(math.e)

PB_AK, PB_AV, PB_BK, PB_BV, PB_CKV, PB_AQ, PB_BQ, PB_CKR, PB_CQ, PB_END = 0, 1, 2, 3, 4, 6, 10, 14, 15, 18
PROJ_COLS = PB_END * LANES

TM_FFN = 512
TM_PROJ = 512
TM_MERGE = 512
TQ_DENSE_SHARED = 512
TQ_DENSE_SPLIT = 512
TQ_WINDOW = 256

_SMEM_SPEC = pl.BlockSpec(memory_space=pltpu.SMEM)
_ANY_SPEC = pl.BlockSpec(memory_space=pl.ANY)


def _proj_weight_natural(w_in):
    def zeros(n):
        return jnp.zeros(w_in.shape[:-1] + (n,), w_in.dtype)

    q0 = KV_COLS
    blocks = [w_in[..., 0:768]]
    for base in (q0, q0 + 512):
        for j in range(4):
            blocks += [w_in[..., base + j * 64:base + (j + 1) * 64],
                       w_in[..., base + (j + 4) * 64:base + (j + 5) * 64]]
    blocks += [zeros(C_NOPE_DIM), w_in[..., 768:800], zeros(32)]
    blocks.append(w_in[..., q0 + 1024:q0 + 1024 + C_Q_RANK])
    return jnp.concatenate(blocks, axis=-1).astype(BF16)


PROJ_BLOCK_KINDS = (("pair", "id") * 2 + ("id",) * 2 + ("pair",) * 8 + ("c",) + ("id",) * 3)


def _pad_heads(w, n_used):
    lead = w.shape[:-1]
    blk = jnp.concatenate([w[..., :n_used], jnp.zeros(lead + (LANES - n_used,), w.dtype)], axis=-1)
    return blk.reshape(lead[:-1] + (N_HEADS * LANES,)).astype(BF16)


def _perm_matrix(kind):
    r = lax.broadcasted_iota(jnp.int32, (LANES, LANES), 0)
    c = lax.broadcasted_iota(jnp.int32, (LANES, LANES), 1)
    if kind == "pair":
        src = ((c >> 5) & 1) * 64 + 2 * (c & 31) + (c >> 6)
    else:
        src = jnp.where(c < 16, 64 + 2 * c,
                        jnp.where(c < 64, c - 16,
                                  jnp.where(c < 80, 2 * c - 63,
                                            jnp.where(c < 96, c - 32, -1))))
    return jnp.where(r == src, 1.0, 0.0).astype(BF16)


def _permute_kernel(w_ref, o_ref, *, kinds):
    mats = {k: _perm_matrix(k) for k in set(kinds) if k != "id"}
    for j, kind in enumerate(kinds):
        sl = slice(j * LANES, (j + 1) * LANES)
        if kind == "id":
            o_ref[:, sl] = w_ref[:, sl]
        else:
            o_ref[:, sl] = _dot(w_ref[:, sl], mats[kind]).astype(BF16)


def _permute_cols(w, kinds):
    n_l, r, c = w.shape
    spec = pl.BlockSpec((None, r, c), lambda l: (l, 0, 0))
    return pl.pallas_call(
        functools.partial(_permute_kernel, kinds=kinds),
        grid=(n_l,),
        in_specs=[spec],
        out_specs=spec,
        out_shape=jax.ShapeDtypeStruct(w.shape, BF16),
        compiler_params=pltpu.CompilerParams(dimension_semantics=("parallel",), vmem_limit_bytes=VMEM_LIMIT),
        name="permute_cols",
    )(w)


def _pair_gain(g):
    e, o = g[:, 0::2], g[:, 1::2]
    return jnp.stack([jnp.concatenate([e, e, o, o], axis=-1), jnp.concatenate([o, o, e, e], axis=-1)], axis=1)


def _c_gain(g):
    nope, rope = g[:, :C_NOPE_DIM], g[:, C_NOPE_DIM:]
    zeros = jnp.zeros((g.shape[0], 32), g.dtype)
    lo = jnp.concatenate([rope[:, 0::2], nope[:, :48]], axis=-1)
    hi = jnp.concatenate([rope[:, 1::2], nope[:, 48:], zeros], axis=-1)
    return jnp.stack([jnp.concatenate([lo, hi], axis=-1), jnp.concatenate([hi, lo], axis=-1)], axis=1)


def _rope_tables(tm):
    pos = np.arange(SEQ)
    row = (pos // GRID_W).astype(np.float32)
    col = (pos % GRID_W).astype(np.float32)

    def angles(rot_dim):
        n_freq = rot_dim // 4
        inv = jnp.asarray(ROPE_THETA, F32) ** (-jnp.arange(n_freq, dtype=F32) / n_freq)
        return jnp.concatenate([jnp.asarray(row)[:, None] * inv, jnp.asarray(col)[:, None] * inv], axis=-1)

    a64 = angles(HEAD_DIM)
    c, s = jnp.cos(a64), jnp.sin(a64)
    cos_ab = jnp.concatenate([c, c, c, c], axis=-1)
    sin_ab = jnp.concatenate([-s, -s, s, s], axis=-1)
    a32 = angles(C_ROPE_DIM)
    c, s = jnp.cos(a32), jnp.sin(a32)
    one48, zero48 = jnp.ones((SEQ, 48), F32), jnp.zeros((SEQ, 48), F32)
    cos_c = jnp.concatenate([c, one48, c, one48], axis=-1)
    sin_c = jnp.concatenate([-s, zero48, s, zero48], axis=-1)
    ident_c, ident_s = jnp.ones((tm, LANES), F32), jnp.zeros((tm, LANES), F32)
    return (jnp.concatenate([cos_ab, ident_c]), jnp.concatenate([sin_ab, ident_s]),
            jnp.concatenate([cos_c, ident_c]), jnp.concatenate([sin_c, ident_s]))


def _adaln_bf16(x, gain, shift, scale):
    inv = lax.rsqrt(jnp.mean(x * x, axis=-1, keepdims=True) + NORM_EPS)
    return ((x * inv) * (gain * (1.0 + scale)) + shift).astype(BF16)


def _rms(x, gain, n):
    return x * lax.rsqrt(jnp.sum(x * x, axis=-1, keepdims=True) * (1.0 / n) + NORM_EPS) * gain


def _rope_gain_tables(g_ref, cos, sin, scale):
    return cos * (g_ref[0:1, :] * scale), sin * (g_ref[1:2, :] * scale)


def _rotate_bf16(y, gcos, gsin):
    return (y * gcos + pltpu.roll(y, 64, axis=1) * gsin).astype(BF16)


def _group_ones(width, group_of):
    r = lax.broadcasted_iota(jnp.int32, (width, width), 0)
    c = lax.broadcasted_iota(jnp.int32, (width, width), 1)
    return jnp.where(group_of(r) == group_of(c), 1.0, 0.0).astype(BF16)


def _inv_norm_mxu(x, ones_mat, n):
    return lax.rsqrt(_dot((x * x).astype(BF16), ones_mat) + n * NORM_EPS)


def _dot(a, b):
    return jnp.dot(a, b, preferred_element_type=F32)


def _dot_nt(a, b):
    return lax.dot_general(a, b, (((1,), (1,)), ((), ())), preferred_element_type=F32)


def _mod_kernel(cond_ref, w_ref, b_ref, o_ref):
    c = cond_ref[...]
    s = (c * jax.nn.sigmoid(c)).astype(BF16)
    o_ref[...] = _dot(s, w_ref[...].astype(BF16)) + b_ref[...]


def _modulation(cond, ada_w, ada_b):
    return pl.pallas_call(
        _mod_kernel,
        grid=(DEPTH, N_MOD),
        in_specs=[
            pl.BlockSpec((GROUP_PAD, D_MODEL), lambda l, k: (0, 0)),
            pl.BlockSpec((None, D_MODEL, D_MODEL), lambda l, k: (l, 0, k)),
            pl.BlockSpec((None, 1, D_MODEL), lambda l, k: (l, 0, k)),
        ],
        out_specs=pl.BlockSpec((None, None, GROUP_PAD, D_MODEL), lambda l, k: (l, k, 0, 0)),
        out_shape=jax.ShapeDtypeStruct((DEPTH, N_MOD, GROUP_PAD, D_MODEL), F32),
        compiler_params=pltpu.CompilerParams(
            dimension_semantics=("arbitrary", "arbitrary"), vmem_limit_bytes=VMEM_LIMIT),
        name="modulation",
    )(cond, ada_w, ada_b.reshape(DEPTH, 1, N_MOD * D_MODEL))


def _mod_spec(layer, k, tm, row0=0):
    base = (layer * N_MOD + k) * GROUP_PAD
    return pl.BlockSpec((1, 1, D_MODEL), lambda i, *_: (base + (row0 + i * tm) // SEQ, 0, 0))


def _ffn_kernel(x_ref, sh_ref, sc_ref, gt_ref, g_ref, wi_ref, wo_ref, *rest):
    o_ref = rest[-1]
    x = x_ref[...]
    h = _adaln_bf16(x, g_ref[...], sh_ref[0], sc_ref[0])
    gate = _dot(h, wi_ref[:, :D_FF])
    up = _dot(h, wi_ref[:, D_FF:])
    act = (gate * jax.nn.sigmoid(gate) * up).astype(BF16)
    o_ref[...] = x + 0.5 * gt_ref[0] * _dot(act, wo_ref[...])


def _ffn(x, mods, layer, k0, gain, w_in, w_out, n_out, row0=0, out_buf=None):
    tm = TM_FFN
    blk0 = row0 // tm
    resident = pl.Buffered(1)
    in_specs = [
        pl.BlockSpec((tm, D_MODEL), lambda i: (i, 0)),
        _mod_spec(layer, k0, tm, row0), _mod_spec(layer, k0 + 1, tm, row0), _mod_spec(layer, k0 + 2, tm, row0),
        pl.BlockSpec((None, 1, D_MODEL), lambda i: (layer, 0, 0)),
        pl.BlockSpec((None, D_MODEL, 2 * D_FF), lambda i: (layer, 0, 0), pipeline_mode=resident),
        pl.BlockSpec((None, D_FF, D_MODEL), lambda i: (layer, 0, 0), pipeline_mode=resident),
    ]
    args = [x, mods, mods, mods, gain, w_in, w_out]
    aliases = {}
    if out_buf is not None:
        in_specs.append(_ANY_SPEC)
        args.append(out_buf)
        aliases = {len(args) - 1: 0}
    return pl.pallas_call(
        _ffn_kernel,
        grid=(x.shape[0] // tm,),
        in_specs=in_specs,
        out_specs=pl.BlockSpec((tm, D_MODEL), lambda i: (blk0 + i, 0)),
        out_shape=jax.ShapeDtypeStruct((n_out, D_MODEL), F32),
        input_output_aliases=aliases,
        compiler_params=pltpu.CompilerParams(
            dimension_semantics=("parallel",), vmem_limit_bytes=VMEM_LIMIT),
        name="ffn",
    )(*args)


ST_KVAB, ST_QA, ST_QB, ST_KC, ST_QC, ST_VC, ST_END = 0, 4, 8, 12, 20, 28, 32


def _proj_kernel(x_ref, sh_ref, sc_ref, g_ref, wp_ref, wkc_ref, wvc_ref, wqc_ref,
                 gak_ref, gaq_ref, gbk_ref, gbq_ref, gkvl_ref, gql_ref, gck_ref, gcq_ref,
                 cab_ref, sab_ref, cc_ref, sc_c_ref,
                 kvab_ref, qa_ref, qb_ref, qc_ref, kc_ref, vc_ref, stage_ref):
    @pl.when(pl.program_id(0) == 0)
    def _():
        stage_ref[...] = jnp.zeros_like(stage_ref)

    def staged(b, n=1):
        return stage_ref[:, b * LANES:(b + n) * LANES]

    def stage(b, val):
        stage_ref[:, b * LANES:b * LANES + val.shape[1]] = val

    cab, sab, cc, sc_c = cab_ref[...], sab_ref[...], cc_ref[...], sc_c_ref[...]
    root_hd, root_c = HEAD_DIM ** 0.5, C_QK_DIM ** 0.5
    for b, g_ref_k in ((0, gak_ref), (2, gbk_ref)):
        gcos, gsin = _rope_gain_tables(g_ref_k, cab, sab, root_hd)
        kvab_ref[:, b * LANES:(b + 1) * LANES] = _rotate_bf16(staged(ST_KVAB + b), gcos, gsin)
        kvab_ref[:, (b + 1) * LANES:(b + 2) * LANES] = staged(ST_KVAB + b + 1).astype(BF16)
    for st, g_ref_q, q_ref in ((ST_QA, gaq_ref, qa_ref), (ST_QB, gbq_ref, qb_ref)):
        gcos, gsin = _rope_gain_tables(g_ref_q, cab, sab, root_hd * HEAD_DIM ** -0.5 * LOG2E)
        for j in range(4):
            q_ref[:, j * LANES:(j + 1) * LANES] = _rotate_bf16(staged(st + j), gcos, gsin)
    gcos_k, gsin_k = _rope_gain_tables(gck_ref, cc, sc_c, root_c)
    gcos_q, gsin_q = _rope_gain_tables(gcq_ref, cc, sc_c, root_c * C_QK_DIM ** -0.5 * LOG2E)
    for hd in range(N_HEADS):
        sl = slice(hd * LANES, (hd + 1) * LANES)
        kc_ref[:, sl] = _rotate_bf16(staged(ST_KC + hd), gcos_k, gsin_k)
        qc_ref[:, sl] = _rotate_bf16(staged(ST_QC + hd), gcos_q, gsin_q)
    vc_ref[...] = staged(ST_VC, 4).astype(BF16)

    h = _adaln_bf16(x_ref[...], g_ref[...], sh_ref[0], sc_ref[0])

    raw = _dot(h, wp_ref[...])

    def project(b, n):
        return raw[:, b * LANES:(b + n) * LANES]

    pair_ones = _group_ones(LANES, lambda l: (l >> 5) & 1)
    pair_ones2 = _group_ones(2 * LANES, lambda l: ((l >> 5) & 1) + 2 * (l >> 7))
    head_ones2 = _group_ones(2 * LANES, lambda l: l >> 7)

    for b in (PB_AK, PB_BK):
        r = project(b, 2)
        k = r[:, :LANES]
        stage(ST_KVAB + b, k * _inv_norm_mxu(k, pair_ones, HEAD_DIM))
        stage(ST_KVAB + b + 1, r[:, LANES:])
    for st, pb in ((ST_QA, PB_AQ), (ST_QB, PB_BQ)):
        for j in (0, 2):
            r = project(pb + j, 2)
            stage(st + j, r * _inv_norm_mxu(r, pair_ones2, HEAD_DIM))

    ckv = _rms(project(PB_CKV, 2), gkvl_ref[...], C_KV_RANK).astype(BF16)
    stage(ST_VC, _dot(ckv, wvc_ref[...]))
    r = project(PB_CKR, 4)
    k_rope2 = jnp.concatenate([r[:, :LANES], r[:, :LANES]], axis=1)
    cq = _rms(r[:, LANES:], gql_ref[...], C_Q_RANK).astype(BF16)
    k_up = _dot(ckv, wkc_ref[...])
    q_up = _dot(cq, wqc_ref[...])
    for hp in range(N_HEADS // 2):
        sl2 = slice(2 * hp * LANES, (2 * hp + 2) * LANES)
        k2 = k_up[:, sl2] + k_rope2
        stage(ST_KC + 2 * hp, k2 * _inv_norm_mxu(k2, head_ones2, C_QK_DIM))
        q2 = q_up[:, sl2]
        stage(ST_QC + 2 * hp, q2 * _inv_norm_mxu(q2, head_ones2, C_QK_DIM))


def _project(x, mods, layer, gain, wp, wkc, wvc, wqc, gains, tables):
    tm = TM_PROJ
    n_tiles = N_ROWS // tm
    n_lat_tiles = N_LAT // tm
    tiles_per_seq = SEQ // tm

    def cur(i):
        return jnp.minimum(i, n_tiles - 1)

    def prev(i):
        return jnp.maximum(i - 1, 0)

    def tab_idx(i):
        t = prev(i)
        return (jnp.where(t < n_lat_tiles, t % tiles_per_seq, tiles_per_seq), 0)

    def mod_spec(k):
        base = (layer * N_MOD + k) * GROUP_PAD
        return pl.BlockSpec((1, 1, D_MODEL), lambda i: (base + (cur(i) * tm) // SEQ, 0, 0))

    def lvec(n):
        return pl.BlockSpec((None, 1, n), lambda i: (layer, 0, 0))

    def lmat(r, c, **kw):
        return pl.BlockSpec((None, r, c), lambda i: (layer, 0, 0), **kw)

    def rows(c):
        return pl.BlockSpec((tm, c), lambda i: (prev(i), 0)), jax.ShapeDtypeStruct((N_ROWS, c), BF16)

    out = [rows(4 * LANES), rows(4 * LANES), rows(4 * LANES), rows(8 * LANES), rows(8 * LANES), rows(4 * LANES)]
    tab_spec = pl.BlockSpec((tm, LANES), tab_idx)
    resident = dict(pipeline_mode=pl.Buffered(1))
    return pl.pallas_call(
        _proj_kernel,
        grid=(n_tiles + 1,),
        in_specs=[
            pl.BlockSpec((tm, D_MODEL), lambda i: (cur(i), 0)),
            mod_spec(3), mod_spec(4),
            lvec(D_MODEL),
            lmat(D_MODEL, PROJ_COLS, **resident), lmat(C_KV_RANK, 8 * LANES, **resident),
            lmat(C_KV_RANK, 4 * LANES, **resident), lmat(C_Q_RANK, 8 * LANES, **resident),
            lmat(2, LANES), lmat(2, LANES), lmat(2, LANES), lmat(2, LANES), lvec(C_KV_RANK), lvec(C_Q_RANK),
            lmat(2, LANES), lmat(2, LANES),
            tab_spec, tab_spec, tab_spec, tab_spec,
        ],
        out_specs=[o[0] for o in out],
        out_shape=[o[1] for o in out],
        scratch_shapes=[pltpu.VMEM((tm, ST_END * LANES), F32)],
        compiler_params=pltpu.CompilerParams(
            dimension_semantics=("arbitrary",), vmem_limit_bytes=VMEM_LIMIT),
        name="project",
    )(x, mods, mods, gain, wp, wkc, wvc, wqc, *gains, *tables)


def _with_ones(v):
    return jnp.concatenate([v, jnp.ones_like(v)], axis=1)


def _attend(q, segs, sink_col):
    scores = []
    for k, _, bias in segs:
        s = _dot_nt(q, k)
        scores.append(s if bias is None else s + bias)
    m = functools.reduce(jnp.maximum, [jnp.max(s, axis=-1, keepdims=True) for s in scores])
    if sink_col is not None:
        m = jnp.maximum(m, sink_col)
    acc = None
    for s, (_, v, _) in zip(scores, segs):
        part = _dot(jnp.exp2(s - m).astype(BF16), v)
        acc = part if acc is None else acc + part
    denom = acc[:, LANES:]
    if sink_col is not None:
        denom = denom + jnp.exp2(sink_col - m)
    return acc[:, :LANES] / denom


def _pair_masks():
    lane = lax.broadcasted_iota(jnp.int32, (1, LANES), 1)
    return (lane & 63) < 32, lane < 64


def _dense_shared_kernel(*refs, layer, tq, n_seg, use_sink):
    q_ref = refs[0]
    kv = refs[1:1 + 2 * n_seg]
    sink_ref = refs[1 + 2 * n_seg] if use_sink else None
    o_ref = refs[-1]
    head0, low = _pair_masks()
    zero = jnp.zeros((), BF16)
    segs = [(kv[2 * s][...], _with_ones(kv[2 * s + 1][...]), None) for s in range(n_seg)]
    for j in range(4):
        qb = q_ref[:, j * LANES:(j + 1) * LANES]
        outs = []
        for half in range(2):
            q = jnp.where(head0, qb, zero) if half == 0 else jnp.where(head0, zero, qb)
            sink_col = (jnp.full((tq, 1), sink_ref[layer, j + 4 * half] * LOG2E, F32) if use_sink else None)
            outs.append(_attend(q, segs, sink_col))
        o_ref[:, j * LANES:(j + 1) * LANES] = jnp.where(low, outs[0], outs[1]).astype(o_ref.dtype)


def _dense_split_kernel(*refs, tq, n_seg):
    q_ref = refs[0]
    kv = refs[1:1 + 2 * n_seg]
    o_ref = refs[-1]
    _, low = _pair_masks()
    for j in range(4):
        vsl = slice(j * LANES, (j + 1) * LANES)
        vals = [_with_ones(kv[2 * s + 1][:, vsl]) for s in range(n_seg)]
        outs = []
        for half in range(2):
            ksl = slice((2 * j + half) * LANES, (2 * j + half + 1) * LANES)
            segs = [(kv[2 * s][:, ksl], vals[s], None) for s in range(n_seg)]
            outs.append(_attend(q_ref[:, ksl], segs, None))
        o_ref[:, vsl] = jnp.where(low, outs[0], outs[1]).astype(o_ref.dtype)


def _window_kernel(q_ref, kl_ref, vl_ref, kc_ref, vc_ref, sink_ref, o_ref, *, layer, tq):
    span = tq + 2 * WINDOW
    start = pl.program_id(1) * tq
    lo = pl.multiple_of(jnp.clip(start - WINDOW, 0, SEQ - span), WINDOW)
    head0, low = _pair_masks()
    zero = jnp.zeros((), BF16)
    q_pos = start + lax.broadcasted_iota(jnp.int32, (tq, span), 0)
    k_pos = lo + lax.broadcasted_iota(jnp.int32, (tq, span), 1)
    band = jnp.where(jnp.abs(q_pos - k_pos) <= WINDOW, 0.0, -jnp.inf)
    band = jnp.concatenate([band, band], axis=0)
    segs = [(kl_ref[pl.ds(lo, span), :], _with_ones(vl_ref[pl.ds(lo, span), :]), band),
            (kc_ref[...], _with_ones(vc_ref[...]), None)]
    for j in range(4):
        qb = q_ref[:, j * LANES:(j + 1) * LANES]
        q = jnp.concatenate([jnp.where(head0, qb, zero), jnp.where(head0, zero, qb)], axis=0)
        sink_col = jnp.concatenate([jnp.full((tq, 1), sink_ref[layer, j] * LOG2E, F32),
                                    jnp.full((tq, 1), sink_ref[layer, j + 4] * LOG2E, F32)], axis=0)
        o = _attend(q, segs, sink_col)
        o_ref[:, j * LANES:(j + 1) * LANES] = jnp.where(low, o[:tq], o[tq:]).astype(o_ref.dtype)


def _attn_params(n_grid):
    return pltpu.CompilerParams(dimension_semantics=("parallel",) * n_grid, vmem_limit_bytes=VMEM_LIMIT)


def _latent_dense_shared(q, kvab, k_blk, v_blk, layer, name, n_out):
    tq = TQ_DENSE_SHARED
    nq = SEQ // tq
    kern = functools.partial(_dense_shared_kernel, layer=layer, tq=tq, n_seg=2, use_sink=False)
    return pl.pallas_call(
        kern,
        grid=(BATCH, nq),
        in_specs=[
            pl.BlockSpec((tq, 4 * LANES), lambda b, i: (b * nq + i, 0)),
            pl.BlockSpec((SEQ, LANES), lambda b, i: (b, k_blk)),
            pl.BlockSpec((SEQ, LANES), lambda b, i: (b, v_blk)),
            pl.BlockSpec((CTX_LEN, LANES), lambda b, i: (CTX_BLK0 + b, k_blk)),
            pl.BlockSpec((CTX_LEN, LANES), lambda b, i: (CTX_BLK0 + b, v_blk)),
        ],
        out_specs=pl.BlockSpec((tq, 4 * LANES), lambda b, i: (b * nq + i, 0)),
        out_shape=jax.ShapeDtypeStruct((n_out, 4 * LANES), BF16),
        compiler_params=_attn_params(2),
        name=name,
    )(q, kvab, kvab, kvab, kvab)


def _latent_window(q, kvab, sink, layer, n_out):
    tq = TQ_WINDOW
    nq = SEQ // tq
    kern = functools.partial(_window_kernel, layer=layer, tq=tq)
    return pl.pallas_call(
        kern,
        grid=(BATCH, nq),
        in_specs=[
            pl.BlockSpec((tq, 4 * LANES), lambda b, i: (b * nq + i, 0)),
            pl.BlockSpec((SEQ, LANES), lambda b, i: (b, 0)),
            pl.BlockSpec((SEQ, LANES), lambda b, i: (b, 1)),
            pl.BlockSpec((CTX_LEN, LANES), lambda b, i: (CTX_BLK0 + b, 0)),
            pl.BlockSpec((CTX_LEN, LANES), lambda b, i: (CTX_BLK0 + b, 1)),
            _SMEM_SPEC,
        ],
        out_specs=pl.BlockSpec((tq, 4 * LANES), lambda b, i: (b * nq + i, 0)),
        out_shape=jax.ShapeDtypeStruct((n_out, 4 * LANES), BF16),
        compiler_params=_attn_params(2),
        name="attn_a_window",
    )(q, kvab, kvab, kvab, kvab, sink)


def _latent_dense_split(q, k, v, n_out):
    tq = TQ_DENSE_SPLIT
    nq = SEQ // tq
    kern = functools.partial(_dense_split_kernel, tq=tq, n_seg=2)
    return pl.pallas_call(
        kern,
        grid=(BATCH, nq),
        in_specs=[
            pl.BlockSpec((tq, 8 * LANES), lambda b, i: (b * nq + i, 0)),
            pl.BlockSpec((SEQ, 8 * LANES), lambda b, i: (b, 0)),
            pl.BlockSpec((SEQ, 4 * LANES), lambda b, i: (b, 0)),
            pl.BlockSpec((CTX_LEN, 8 * LANES), lambda b, i: (CTX_BLK0 + b, 0)),
            pl.BlockSpec((CTX_LEN, 4 * LANES), lambda b, i: (CTX_BLK0 + b, 0)),
        ],
        out_specs=pl.BlockSpec((tq, 4 * LANES), lambda b, i: (b * nq + i, 0)),
        out_shape=jax.ShapeDtypeStruct((n_out, 4 * LANES), BF16),
        compiler_params=_attn_params(2),
        name="attn_c_latent",
    )(q, k, v, k, v)


def _context_shared(y, q, kvab, k_blk, v_blk, sink, layer, name):
    tq = CTX_LEN
    use_sink = sink is not None
    kern = functools.partial(_dense_shared_kernel, layer=layer, tq=tq, n_seg=1, use_sink=use_sink)
    in_specs = [
        pl.BlockSpec((tq, 4 * LANES), lambda b: (CTX_BLK0 + b, 0)),
        pl.BlockSpec((CTX_LEN, LANES), lambda b: (CTX_BLK0 + b, k_blk)),
        pl.BlockSpec((CTX_LEN, LANES), lambda b: (CTX_BLK0 + b, v_blk)),
    ]
    args = [q, kvab, kvab]
    if use_sink:
        in_specs.append(_SMEM_SPEC)
        args.append(sink)
    in_specs.append(_ANY_SPEC)
    args.append(y)
    return pl.pallas_call(
        kern,
        grid=(BATCH,),
        in_specs=in_specs,
        out_specs=pl.BlockSpec((tq, 4 * LANES), lambda b: (CTX_BLK0 + b, 0)),
        out_shape=jax.ShapeDtypeStruct((N_ROWS, 4 * LANES), BF16),
        input_output_aliases={len(args) - 1: 0},
        compiler_params=_attn_params(1),
        name=name,
    )(*args)


def _context_split(y, q, k, v):
    tq = CTX_LEN
    kern = functools.partial(_dense_split_kernel, tq=tq, n_seg=1)
    return pl.pallas_call(
        kern,
        grid=(BATCH,),
        in_specs=[
            pl.BlockSpec((tq, 8 * LANES), lambda b: (CTX_BLK0 + b, 0)),
            pl.BlockSpec((CTX_LEN, 8 * LANES), lambda b: (CTX_BLK0 + b, 0)),
            pl.BlockSpec((CTX_LEN, 4 * LANES), lambda b: (CTX_BLK0 + b, 0)),
            _ANY_SPEC,
        ],
        out_specs=pl.BlockSpec((tq, 4 * LANES), lambda b: (CTX_BLK0 + b, 0)),
        out_shape=jax.ShapeDtypeStruct((N_ROWS, 4 * LANES), BF16),
        input_output_aliases={3: 0},
        compiler_params=_attn_params(1),
        name="attn_c_context",
    )(q, k, v, y)


def _merge_kernel(x_ref, sh_ref, sc_ref, gt_ref, g_ref, wg_ref, ya_ref, yb_ref, yc_ref,
                  wa_ref, wb_ref, wc_ref, wo_ref, o_ref):
    x = x_ref[...]
    u = _adaln_bf16(x, g_ref[...], sh_ref[0], sc_ref[0])
    m = None
    for idx, (y_ref, w_ref) in enumerate(((ya_ref, wa_ref), (yb_ref, wb_ref), (yc_ref, wc_ref))):
        gate = _dot(u, wg_ref[:, idx * D_MODEL:(idx + 1) * D_MODEL])
        term = jax.nn.sigmoid(gate) * _dot(y_ref[...], w_ref[...])
        m = term if m is None else m + term
    o_ref[...] = x + gt_ref[0] * _dot(m.astype(BF16), wo_ref[...])


def _merge(x, mods, layer, gain, wg, ya, yb, yc, wa, wb, wc, wo, n_rows):
    tm = TM_MERGE

    def lmat(r, c):
        return pl.BlockSpec((None, r, c), lambda i: (layer, 0, 0))

    y_spec = pl.BlockSpec((tm, 4 * LANES), lambda i: (i, 0))
    return pl.pallas_call(
        _merge_kernel,
        grid=(n_rows // tm,),
        in_specs=[
            pl.BlockSpec((tm, D_MODEL), lambda i: (i, 0)),
            _mod_spec(layer, 3, tm), _mod_spec(layer, 4, tm), _mod_spec(layer, 5, tm),
            pl.BlockSpec((None, 1, D_MODEL), lambda i: (layer, 0, 0)),
            lmat(D_MODEL, GATE_COLS),
            y_spec, y_spec, y_spec,
            lmat(4 * LANES, D_MODEL), lmat(4 * LANES, D_MODEL), lmat(4 * LANES, D_MODEL),
            lmat(D_MODEL, D_MODEL),
        ],
        out_specs=pl.BlockSpec((tm, D_MODEL), lambda i: (i, 0)),
        out_shape=jax.ShapeDtypeStruct((n_rows, D_MODEL), F32),
        compiler_params=pltpu.CompilerParams(
            dimension_semantics=("parallel",), vmem_limit_bytes=VMEM_LIMIT),
        name="merge",
    )(x, mods, mods, mods, gain, wg, ya, yb, yc, wa, wb, wc, wo)


def kernel(x, c, ctx, c_ctx, ada_w, ada_b, ffn1_norm, ffn1_w_in, ffn1_w_out, mix_norm, mix_w_in, a_q_norm, a_k_norm, a_sink, b_q_norm, b_k_norm, c_q_lat_norm, c_w_uq, c_kv_lat_norm, c_w_ukv, c_q_norm, c_k_norm, w_branch_a, w_branch_b, w_branch_c, mix_w_out, ffn2_norm, ffn2_w_in, ffn2_w_out):
    n_l = mix_w_in.shape[0]
    wp = _permute_cols(_proj_weight_natural(mix_w_in), PROJ_BLOCK_KINDS)
    wg = mix_w_in[:, :, KV_COLS + Q_COLS:].astype(BF16)
    ukv = c_w_ukv.reshape(n_l, C_KV_RANK, N_HEADS, C_NOPE_DIM + C_V_DIM)
    wkc = _permute_cols(_pad_heads(ukv, C_NOPE_DIM), ("c",) * N_HEADS)
    wvc = ukv[..., C_NOPE_DIM:].reshape(n_l, C_KV_RANK, N_HEADS * C_V_DIM).astype(BF16)
    uq = c_w_uq.reshape(n_l, C_Q_RANK, N_HEADS, C_QK_DIM)
    wqc = _permute_cols(_pad_heads(uq, C_QK_DIM), ("c",) * N_HEADS)

    def pair_rows(w):
        return w.reshape(n_l, 2, 4, HEAD_DIM, D_MODEL).transpose(0, 2, 1, 3, 4).reshape(n_l, 8 * HEAD_DIM, D_MODEL)

    wa = pair_rows(w_branch_a).astype(BF16)
    wb = pair_rows(w_branch_b).astype(BF16)
    wc = w_branch_c.astype(BF16)
    wo = mix_w_out.astype(BF16)
    f1_in, f1_out = ffn1_w_in.astype(BF16), ffn1_w_out.astype(BF16)
    f2_in, f2_out = ffn2_w_in.astype(BF16), ffn2_w_out.astype(BF16)
    gains = (_pair_gain(a_k_norm), _pair_gain(a_q_norm), _pair_gain(b_k_norm), _pair_gain(b_q_norm),
             c_kv_lat_norm[:, None, :], c_q_lat_norm[:, None, :], _c_gain(c_k_norm), _c_gain(c_q_norm))
    tables = _rope_tables(TM_PROJ)
    ffn1_g, mix_g, ffn2_g = ffn1_norm[:, None, :], mix_norm[:, None, :], ffn2_norm[:, None, :]

    cond = jnp.concatenate([c, c_ctx[None, :], jnp.zeros((GROUP_PAD - N_GROUPS, D_MODEL), F32)], axis=0)
    mods = _modulation(cond, ada_w, ada_b).reshape(DEPTH * N_MOD * GROUP_PAD, 1, D_MODEL)

    xs = None
    for l in range(DEPTH):
        last = l == DEPTH - 1
        if l == 0:
            xs = _ffn(x.reshape(N_LAT, D_MODEL), mods, l, 0, ffn1_g, f1_in, f1_out, N_ROWS)
            xs = _ffn(ctx.reshape(N_CTX, D_MODEL), mods, l, 0, ffn1_g, f1_in, f1_out, N_ROWS, N_LAT, xs)
        else:
            xs = _ffn(xs, mods, l, 0, ffn1_g, f1_in, f1_out, N_ROWS)
        kvab, qa, qb, qc, kc, vc = _project(xs, mods, l, mix_g, wp, wkc, wvc, wqc, gains, tables)
        n_rows = N_LAT if last else N_ROWS
        ya = _latent_window(qa, kvab, a_sink, l, n_rows)
        yb = _latent_dense_shared(qb, kvab, 2, 3, l, "attn_b_latent", n_rows)
        yc = _latent_dense_split(qc, kc, vc, n_rows)
        if not last:
            ya = _context_shared(ya, qa, kvab, 0, 1, a_sink, l, "attn_a_context")
            yb = _context_shared(yb, qb, kvab, 2, 3, None, l, "attn_b_context")
            yc = _context_split(yc, qc, kc, vc)
        xs = _merge(xs, mods, l, mix_g, wg, ya, yb, yc, wa, wb, wc, wo, n_rows)
        xs = _ffn(xs, mods, l, 6, ffn2_g, f2_in, f2_out, n_rows)
    return xs.reshape(BATCH, SEQ, D_MODEL)
```

```python
import functools
import math

import numpy as np
import jax
import jax.numpy as jnp
from jax import lax
from jax.experimental import pallas as pl
from jax.experimental.pallas import tpu as pltpu

F32 = jnp.float32
BF16 = jnp.bfloat16

D_MODEL = 1024
BATCH = 8
SEQ = 2048
DEPTH = 2
GRID_W = 64
CTX_LEN = 256
HEAD_DIM = 64
WINDOW = 128
ROPE_THETA = 10000.0
NORM_EPS = 1e-6
N_MOD = 9
D_FF = 2816
N_HEADS = 8
C_NOPE_DIM = 64
C_ROPE_DIM = 32
C_QK_DIM = C_NOPE_DIM + C_ROPE_DIM
C_V_DIM = 64
C_Q_RANK = 384
C_KV_RANK = 256
KV_COLS = 800
Q_COLS = 1408
GATE_COLS = 3 * D_MODEL

LANES = 128
N_LAT = BATCH * SEQ
N_CTX = BATCH * CTX_LEN
N_ROWS = N_LAT + N_CTX
N_GROUPS = BATCH + 1
GROUP_PAD = 16
CTX_BLK0 = N_LAT // CTX_LEN

VMEM_LIMIT = 56 * 1024 * 1024
LOG2E = math.log2(math.e)

PB_AK, PB_AV, PB_BK, PB_BV, PB_CKV, PB_AQ, PB_BQ, PB_CKR, PB_CQ, PB_END = 0, 1, 2, 3, 4, 6, 10, 14, 15, 18
PROJ_COLS = PB_END * LANES

TM_FFN = 512
TM_PROJ = 512
TM_MERGE = 1024
TQ_DENSE_SHARED = 1024
TQ_DENSE_SPLIT = 1024
TQ_WINDOW = 256
WINDOW_SUBTILES = 4

_SMEM_SPEC = pl.BlockSpec(memory_space=pltpu.SMEM)


def _proj_weight_natural(w_in):
    def zeros(n):
        return jnp.zeros(w_in.shape[:-1] + (n,), w_in.dtype)

    q0 = KV_COLS
    blocks = [w_in[..., 0:768]]
    for base in (q0, q0 + 512):
        for j in range(4):
            blocks += [w_in[..., base + j * 64:base + (j + 1) * 64],
                       w_in[..., base + (j + 4) * 64:base + (j + 5) * 64]]
    blocks += [zeros(C_NOPE_DIM), w_in[..., 768:800], zeros(32)]
    blocks.append(w_in[..., q0 + 1024:q0 + 1024 + C_Q_RANK])
    return jnp.concatenate(blocks, axis=-1).astype(BF16)


PROJ_BLOCK_KINDS = (("pair", "id") * 2 + ("id",) * 2 + ("pair",) * 8 + ("c",) + ("id",) * 3)


def _pad_heads(w, n_used):
    lead = w.shape[:-1]
    blk = jnp.concatenate([w[..., :n_used], jnp.zeros(lead + (LANES - n_used,), w.dtype)], axis=-1)
    return blk.reshape(lead[:-1] + (N_HEADS * LANES,)).astype(BF16)


def _perm_matrix(kind):
    r = lax.broadcasted_iota(jnp.int32, (LANES, LANES), 0)
    c = lax.broadcasted_iota(jnp.int32, (LANES, LANES), 1)
    if kind == "pair":
        src = ((c >> 5) & 1) * 64 + 2 * (c & 31) + (c >> 6)
    else:
        src = jnp.where(c < 16, 64 + 2 * c,
                        jnp.where(c < 64, c - 16,
                                  jnp.where(c < 80, 2 * c - 63,
                                            jnp.where(c < 96, c - 32, -1))))
    return jnp.where(r == src, 1.0, 0.0).astype(BF16)


def _permute_kernel(w_ref, o_ref, *, kinds):
    mats = {k: _perm_matrix(k) for k in set(kinds) if k != "id"}
    for j, kind in enumerate(kinds):
        sl = slice(j * LANES, (j + 1) * LANES)
        if kind == "id":
            o_ref[:, sl] = w_ref[:, sl]
        else:
            o_ref[:, sl] = _dot(w_ref[:, sl], mats[kind]).astype(BF16)


def _permute_cols(w, kinds):
    n_l, r, c = w.shape
    spec = pl.BlockSpec((None, r, c), lambda l: (l, 0, 0))
    return pl.pallas_call(
        functools.partial(_permute_kernel, kinds=kinds),
        grid=(n_l,),
        in_specs=[spec],
        out_specs=spec,
        out_shape=jax.ShapeDtypeStruct(w.shape, BF16),
        compiler_params=pltpu.CompilerParams(dimension_semantics=("parallel",), vmem_limit_bytes=VMEM_LIMIT),
        name="permute_cols",
    )(w)


def _pair_gain(g):
    e, o = g[:, 0::2], g[:, 1::2]
    return jnp.stack([jnp.concatenate([e, e, o, o], axis=-1), jnp.concatenate([o, o, e, e], axis=-1)], axis=1)


def _c_gain(g):
    nope, rope = g[:, :C_NOPE_DIM], g[:, C_NOPE_DIM:]
    zeros = jnp.zeros((g.shape[0], 32), g.dtype)
    lo = jnp.concatenate([rope[:, 0::2], nope[:, :48]], axis=-1)
    hi = jnp.concatenate([rope[:, 1::2], nope[:, 48:], zeros], axis=-1)
    return jnp.stack([jnp.concatenate([lo, hi], axis=-1), jnp.concatenate([hi, lo], axis=-1)], axis=1)


def _rope_tables(tm):
    pos = np.arange(SEQ)
    row = (pos // GRID_W).astype(np.float32)
    col = (pos % GRID_W).astype(np.float32)

    def angles(rot_dim):
        n_freq = rot_dim // 4
        inv = jnp.asarray(ROPE_THETA, F32) ** (-jnp.arange(n_freq, dtype=F32) / n_freq)
        return jnp.concatenate([jnp.asarray(row)[:, None] * inv, jnp.asarray(col)[:, None] * inv], axis=-1)

    a64 = angles(HEAD_DIM)
    c, s = jnp.cos(a64), jnp.sin(a64)
    cos_ab = jnp.concatenate([c, c, c, c], axis=-1)
    sin_ab = jnp.concatenate([-s, -s, s, s], axis=-1)
    a32 = angles(C_ROPE_DIM)
    c, s = jnp.cos(a32), jnp.sin(a32)
    one48, zero48 = jnp.ones((SEQ, 48), F32), jnp.zeros((SEQ, 48), F32)
    cos_c = jnp.concatenate([c, one48, c, one48], axis=-1)
    sin_c = jnp.concatenate([-s, zero48, s, zero48], axis=-1)
    ident_c, ident_s = jnp.ones((tm, LANES), F32), jnp.zeros((tm, LANES), F32)
    return (jnp.concatenate([cos_ab, ident_c]), jnp.concatenate([sin_ab, ident_s]),
            jnp.concatenate([cos_c, ident_c]), jnp.concatenate([sin_c, ident_s]))


def _adaln_bf16(x, gain, shift, scale):
    inv = lax.rsqrt(jnp.mean(x * x, axis=-1, keepdims=True) + NORM_EPS)
    return ((x * inv) * (gain * (1.0 + scale)) + shift).astype(BF16)


def _rms(x, gain, n):
    return x * lax.rsqrt(jnp.sum(x * x, axis=-1, keepdims=True) * (1.0 / n) + NORM_EPS) * gain


def _rope_gain_tables(g_ref, cos, sin, scale):
    return cos * (g_ref[0:1, :] * scale), sin * (g_ref[1:2, :] * scale)


def _rotate_bf16(y, gcos, gsin):
    return (y * gcos + pltpu.roll(y, 64, axis=1) * gsin).astype(BF16)


def _group_ones(width, group_of):
    r = lax.broadcasted_iota(jnp.int32, (width, width), 0)
    c = lax.broadcasted_iota(jnp.int32, (width, width), 1)
    return jnp.where(group_of(r) == group_of(c), 1.0, 0.0).astype(BF16)


def _inv_norm_mxu(x, ones_mat, n):
    return lax.rsqrt(_dot((x * x).astype(BF16), ones_mat) + n * NORM_EPS)


def _dot(a, b):
    return jnp.dot(a, b, preferred_element_type=F32)


def _dot_nt(a, b):
    return lax.dot_general(a, b, (((1,), (1,)), ((), ())), preferred_element_type=F32)


def _mod_kernel(cond_ref, w_ref, b_ref, o_ref):
    c = cond_ref[...]
    s = (c * jax.nn.sigmoid(c)).astype(BF16)
    o_ref[...] = _dot(s, w_ref[...].astype(BF16)) + b_ref[...]


def _modulation(cond, ada_w, ada_b):
    return pl.pallas_call(
        _mod_kernel,
        grid=(DEPTH, N_MOD),
        in_specs=[
            pl.BlockSpec((GROUP_PAD, D_MODEL), lambda l, k: (0, 0)),
            pl.BlockSpec((None, D_MODEL, D_MODEL), lambda l, k: (l, 0, k)),
            pl.BlockSpec((None, 1, D_MODEL), lambda l, k: (l, 0, k)),
        ],
        out_specs=pl.BlockSpec((None, None, GROUP_PAD, D_MODEL), lambda l, k: (l, k, 0, 0)),
        out_shape=jax.ShapeDtypeStruct((DEPTH, N_MOD, GROUP_PAD, D_MODEL), F32),
        compiler_params=pltpu.CompilerParams(
            dimension_semantics=("arbitrary", "arbitrary"), vmem_limit_bytes=VMEM_LIMIT),
        name="modulation",
    )(cond, ada_w, ada_b.reshape(DEPTH, 1, N_MOD * D_MODEL))


def _mod_spec(layer, k, tm):
    base = (layer * N_MOD + k) * GROUP_PAD
    return pl.BlockSpec((1, 1, D_MODEL), lambda i: (base + (i * tm) // SEQ, 0, 0))


def _split_row_specs(tm, cols):
    n_lat = N_LAT // tm
    return (pl.BlockSpec((tm, cols), lambda i: (jnp.minimum(i, n_lat - 1), 0)),
            pl.BlockSpec((tm, cols), lambda i: (jnp.maximum(i - n_lat, 0), 0)))


def _pick_rows(lat_ref, ctx_ref, tm):
    return jnp.where(pl.program_id(0) >= N_LAT // tm, ctx_ref[...], lat_ref[...])


def _ffn_kernel(*refs, split):
    if split:
        x = _pick_rows(refs[0], refs[1], TM_FFN)
        refs = refs[1:]
    else:
        x = refs[0][...]
    _, sh_ref, sc_ref, gt_ref, g_ref, wi_ref, wo_ref, o_ref = refs
    h = _adaln_bf16(x, g_ref[...], sh_ref[0], sc_ref[0])
    gate = _dot(h, wi_ref[:, :D_FF])
    up = _dot(h, wi_ref[:, D_FF:])
    act = (gate * jax.nn.sigmoid(gate) * up).astype(BF16)
    o_ref[...] = x + 0.5 * gt_ref[0] * _dot(act, wo_ref[...])


def _ffn(x, mods, layer, k0, gain, w_in, w_out, x_ctx=None):
    tm = TM_FFN
    split = x_ctx is not None
    n_rows = x.shape[0] + (x_ctx.shape[0] if split else 0)
    resident = pl.Buffered(1)
    x_specs = list(_split_row_specs(tm, D_MODEL)) if split else [pl.BlockSpec((tm, D_MODEL), lambda i: (i, 0))]
    return pl.pallas_call(
        functools.partial(_ffn_kernel, split=split),
        grid=(n_rows // tm,),
        in_specs=x_specs + [
            _mod_spec(layer, k0, tm), _mod_spec(layer, k0 + 1, tm), _mod_spec(layer, k0 + 2, tm),
            pl.BlockSpec((None, 1, D_MODEL), lambda i: (layer, 0, 0)),
            pl.BlockSpec((None, D_MODEL, 2 * D_FF), lambda i: (layer, 0, 0), pipeline_mode=resident),
            pl.BlockSpec((None, D_FF, D_MODEL), lambda i: (layer, 0, 0), pipeline_mode=resident),
        ],
        out_specs=pl.BlockSpec((tm, D_MODEL), lambda i: (i, 0)),
        out_shape=jax.ShapeDtypeStruct((n_rows, D_MODEL), F32),
        compiler_params=pltpu.CompilerParams(
            dimension_semantics=("parallel",), vmem_limit_bytes=VMEM_LIMIT),
        name="ffn",
    )(*([x, x_ctx] if split else [x]), mods, mods, mods, gain, w_in, w_out)


ST_KVAB, ST_QA, ST_QB, ST_KC, ST_QC, ST_VC, ST_END = 0, 4, 8, 12, 20, 28, 32


def _proj_kernel(x_ref, sh_ref, sc_ref, g_ref, wp_ref, wkc_ref, wvc_ref, wqc_ref,
                 gak_ref, gaq_ref, gbk_ref, gbq_ref, gkvl_ref, gql_ref, gck_ref, gcq_ref,
                 cab_ref, sab_ref, cc_ref, sc_c_ref,
                 kvab_ref, qa_ref, qb_ref, qc_ref, kc_ref, vc_ref, stage_ref):
    @pl.when(pl.program_id(0) == 0)
    def _():
        stage_ref[...] = jnp.zeros_like(stage_ref)

    def staged(b, n=1):
        return stage_ref[:, b * LANES:(b + n) * LANES]

    def stage(b, val):
        stage_ref[:, b * LANES:b * LANES + val.shape[1]] = val

    cab, sab, cc, sc_c = cab_ref[...], sab_ref[...], cc_ref[...], sc_c_ref[...]
    root_hd, root_c = HEAD_DIM ** 0.5, C_QK_DIM ** 0.5
    for b, g_ref_k in ((0, gak_ref), (2, gbk_ref)):
        gcos, gsin = _rope_gain_tables(g_ref_k, cab, sab, root_hd)
        kvab_ref[:, b * LANES:(b + 1) * LANES] = _rotate_bf16(staged(ST_KVAB + b), gcos, gsin)
        kvab_ref[:, (b + 1) * LANES:(b + 2) * LANES] = staged(ST_KVAB + b + 1).astype(BF16)
    for st, g_ref_q, q_ref in ((ST_QA, gaq_ref, qa_ref), (ST_QB, gbq_ref, qb_ref)):
        gcos, gsin = _rope_gain_tables(g_ref_q, cab, sab, root_hd * HEAD_DIM ** -0.5 * LOG2E)
        for j in range(4):
            q_ref[:, j * LANES:(j + 1) * LANES] = _rotate_bf16(staged(st + j), gcos, gsin)
    gcos_k, gsin_k = _rope_gain_tables(gck_ref, cc, sc_c, root_c)
    gcos_q, gsin_q = _rope_gain_tables(gcq_ref, cc, sc_c, root_c * C_QK_DIM ** -0.5 * LOG2E)
    for hd in range(N_HEADS):
        sl = slice(hd * LANES, (hd + 1) * LANES)
        kc_ref[:, sl] = _rotate_bf16(staged(ST_KC + hd), gcos_k, gsin_k)
        qc_ref[:, sl] = _rotate_bf16(staged(ST_QC + hd), gcos_q, gsin_q)
    vc_ref[...] = staged(ST_VC, 4).astype(BF16)

    h = _adaln_bf16(x_ref[...], g_ref[...], sh_ref[0], sc_ref[0])

    raw = _dot(h, wp_ref[...])

    def project(b, n):
        return raw[:, b * LANES:(b + n) * LANES]

    pair_ones = _group_ones(LANES, lambda l: (l >> 5) & 1)
    pair_ones2 = _group_ones(2 * LANES, lambda l: ((l >> 5) & 1) + 2 * (l >> 7))
    head_ones2 = _group_ones(2 * LANES, lambda l: l >> 7)

    for b in (PB_AK, PB_BK):
        r = project(b, 2)
        k = r[:, :LANES]
        stage(ST_KVAB + b, k * _inv_norm_mxu(k, pair_ones, HEAD_DIM))
        stage(ST_KVAB + b + 1, r[:, LANES:])
    for st, pb in ((ST_QA, PB_AQ), (ST_QB, PB_BQ)):
        for j in (0, 2):
            r = project(pb + j, 2)
            stage(st + j, r * _inv_norm_mxu(r, pair_ones2, HEAD_DIM))

    ckv = _rms(project(PB_CKV, 2), gkvl_ref[...], C_KV_RANK).astype(BF16)
    stage(ST_VC, _dot(ckv, wvc_ref[...]))
    r = project(PB_CKR, 4)
    k_rope2 = jnp.concatenate([r[:, :LANES], r[:, :LANES]], axis=1)
    cq = _rms(r[:, LANES:], gql_ref[...], C_Q_RANK).astype(BF16)
    k_up = _dot(ckv, wkc_ref[...])
    q_up = _dot(cq, wqc_ref[...])
    for hp in range(N_HEADS // 2):
        sl2 = slice(2 * hp * LANES, (2 * hp + 2) * LANES)
        k2 = k_up[:, sl2] + k_rope2
        stage(ST_KC + 2 * hp, k2 * _inv_norm_mxu(k2, head_ones2, C_QK_DIM))
        q2 = q_up[:, sl2]
        stage(ST_QC + 2 * hp, q2 * _inv_norm_mxu(q2, head_ones2, C_QK_DIM))


def _project(x, mods, layer, gain, wp, wkc, wvc, wqc, gains, tables):
    tm = TM_PROJ
    n_tiles = N_ROWS // tm
    n_lat_tiles = N_LAT // tm
    tiles_per_seq = SEQ // tm

    def cur(i):
        return jnp.minimum(i, n_tiles - 1)

    def prev(i):
        return jnp.maximum(i - 1, 0)

    def tab_idx(i):
        t = prev(i)
        return (jnp.where(t < n_lat_tiles, t % tiles_per_seq, tiles_per_seq), 0)

    def mod_spec(k):
        base = (layer * N_MOD + k) * GROUP_PAD
        return pl.BlockSpec((1, 1, D_MODEL), lambda i: (base + (cur(i) * tm) // SEQ, 0, 0))

    def lvec(n):
        return pl.BlockSpec((None, 1, n), lambda i: (layer, 0, 0))

    def lmat(r, c, **kw):
        return pl.BlockSpec((None, r, c), lambda i: (layer, 0, 0), **kw)

    def rows(c):
        return pl.BlockSpec((tm, c), lambda i: (prev(i), 0)), jax.ShapeDtypeStruct((N_ROWS, c), BF16)

    out = [rows(4 * LANES), rows(4 * LANES), rows(4 * LANES), rows(8 * LANES), rows(8 * LANES), rows(4 * LANES)]
    tab_spec = pl.BlockSpec((tm, LANES), tab_idx)
    resident = dict(pipeline_mode=pl.Buffered(1))
    return pl.pallas_call(
        _proj_kernel,
        grid=(n_tiles + 1,),
        in_specs=[
            pl.BlockSpec((tm, D_MODEL), lambda i: (cur(i), 0)),
            mod_spec(3), mod_spec(4),
            lvec(D_MODEL),
            lmat(D_MODEL, PROJ_COLS, **resident), lmat(C_KV_RANK, 8 * LANES, **resident),
            lmat(C_KV_RANK, 4 * LANES, **resident), lmat(C_Q_RANK, 8 * LANES, **resident),
            lmat(2, LANES), lmat(2, LANES), lmat(2, LANES), lmat(2, LANES), lvec(C_KV_RANK), lvec(C_Q_RANK),
            lmat(2, LANES), lmat(2, LANES),
            tab_spec, tab_spec, tab_spec, tab_spec,
        ],
        out_specs=[o[0] for o in out],
        out_shape=[o[1] for o in out],
        scratch_shapes=[pltpu.VMEM((tm, ST_END * LANES), F32)],
        compiler_params=pltpu.CompilerParams(
            dimension_semantics=("arbitrary",), vmem_limit_bytes=VMEM_LIMIT),
        name="project",
    )(x, mods, mods, gain, wp, wkc, wvc, wqc, *gains, *tables)


def _with_ones(v):
    return jnp.concatenate([v, jnp.ones_like(v)], axis=1)


def _attend(q, segs, sink_col):
    scores = []
    for k, _, bias in segs:
        s = _dot_nt(q, k)
        scores.append(s if bias is None else s + bias)
    m = functools.reduce(jnp.maximum, [jnp.max(s, axis=-1, keepdims=True) for s in scores])
    if sink_col is not None:
        m = jnp.maximum(m, sink_col)
    acc = None
    for s, (_, v, _) in zip(scores, segs):
        part = _dot(jnp.exp2(s - m).astype(BF16), v)
        acc = part if acc is None else acc + part
    denom = acc[:, LANES:]
    if sink_col is not None:
        denom = denom + jnp.exp2(sink_col - m)
    return acc[:, :LANES] / denom


def _pair_masks():
    lane = lax.broadcasted_iota(jnp.int32, (1, LANES), 1)
    return (lane & 63) < 32, lane < 64


def _dense_shared_kernel(*refs, layer, tq, n_seg, use_sink):
    q_ref = refs[0]
    kv = refs[1:1 + 2 * n_seg]
    sink_ref = refs[1 + 2 * n_seg] if use_sink else None
    o_ref = refs[-1]
    head0, low = _pair_masks()
    zero = jnp.zeros((), BF16)
    segs = [(kv[2 * s][...], _with_ones(kv[2 * s + 1][...]), None) for s in range(n_seg)]
    for j in range(4):
        qb = q_ref[:, j * LANES:(j + 1) * LANES]
        outs = []
        for half in range(2):
            q = jnp.where(head0, qb, zero) if half == 0 else jnp.where(head0, zero, qb)
            sink_col = (jnp.full((tq, 1), sink_ref[layer, j + 4 * half] * LOG2E, F32) if use_sink else None)
            outs.append(_attend(q, segs, sink_col))
        o_ref[:, j * LANES:(j + 1) * LANES] = jnp.where(low, outs[0], outs[1]).astype(o_ref.dtype)


def _dense_split_kernel(*refs, tq, n_seg):
    q_ref = refs[0]
    kv = refs[1:1 + 2 * n_seg]
    o_ref = refs[-1]
    _, low = _pair_masks()
    for j in range(4):
        vsl = slice(j * LANES, (j + 1) * LANES)
        vals = [_with_ones(kv[2 * s + 1][:, vsl]) for s in range(n_seg)]
        outs = []
        for half in range(2):
            ksl = slice((2 * j + half) * LANES, (2 * j + half + 1) * LANES)
            segs = [(kv[2 * s][:, ksl], vals[s], None) for s in range(n_seg)]
            outs.append(_attend(q_ref[:, ksl], segs, None))
        o_ref[:, vsl] = jnp.where(low, outs[0], outs[1]).astype(o_ref.dtype)


def _window_kernel(q_ref, kl_ref, vl_ref, kc_ref, vc_ref, sink_ref, o_ref, *, layer, tq, n_sub):
    span = tq + 2 * WINDOW
    head0, low = _pair_masks()
    zero = jnp.zeros((), BF16)
    ctx_seg = (kc_ref[...], _with_ones(vc_ref[...]), None)
    for sub in range(n_sub):
        rows = slice(sub * tq, (sub + 1) * tq)
        start = (pl.program_id(1) * n_sub + sub) * tq
        lo = pl.multiple_of(jnp.clip(start - WINDOW, 0, SEQ - span), WINDOW)
        q_pos = start + lax.broadcasted_iota(jnp.int32, (tq, span), 0)
        k_pos = lo + lax.broadcasted_iota(jnp.int32, (tq, span), 1)
        band = jnp.where(jnp.abs(q_pos - k_pos) <= WINDOW, 0.0, -jnp.inf)
        band = jnp.concatenate([band, band], axis=0)
        segs = [(kl_ref[pl.ds(lo, span), :], _with_ones(vl_ref[pl.ds(lo, span), :]), band), ctx_seg]
        for j in range(4):
            qb = q_ref[rows, j * LANES:(j + 1) * LANES]
            q = jnp.concatenate([jnp.where(head0, qb, zero), jnp.where(head0, zero, qb)], axis=0)
            sink_col = jnp.concatenate([jnp.full((tq, 1), sink_ref[layer, j] * LOG2E, F32),
                                        jnp.full((tq, 1), sink_ref[layer, j + 4] * LOG2E, F32)], axis=0)
            o = _attend(q, segs, sink_col)
            o_ref[rows, j * LANES:(j + 1) * LANES] = jnp.where(low, o[:tq], o[tq:]).astype(o_ref.dtype)


def _attn_params(n_grid):
    return pltpu.CompilerParams(dimension_semantics=("parallel",) * n_grid, vmem_limit_bytes=VMEM_LIMIT)


def _latent_dense_shared(q, kvab, k_blk, v_blk, layer, name):
    tq = TQ_DENSE_SHARED
    nq = SEQ // tq
    kern = functools.partial(_dense_shared_kernel, layer=layer, tq=tq, n_seg=2, use_sink=False)
    return pl.pallas_call(
        kern,
        grid=(BATCH, nq),
        in_specs=[
            pl.BlockSpec((tq, 4 * LANES), lambda b, i: (b * nq + i, 0)),
            pl.BlockSpec((SEQ, LANES), lambda b, i: (b, k_blk)),
            pl.BlockSpec((SEQ, LANES), lambda b, i: (b, v_blk)),
            pl.BlockSpec((CTX_LEN, LANES), lambda b, i: (CTX_BLK0 + b, k_blk)),
            pl.BlockSpec((CTX_LEN, LANES), lambda b, i: (CTX_BLK0 + b, v_blk)),
        ],
        out_specs=pl.BlockSpec((tq, 4 * LANES), lambda b, i: (b * nq + i, 0)),
        out_shape=jax.ShapeDtypeStruct((N_LAT, 4 * LANES), BF16),
        compiler_params=_attn_params(2),
        name=name,
    )(q, kvab, kvab, kvab, kvab)


def _latent_window(q, kvab, sink, layer):
    n_sub = WINDOW_SUBTILES
    tq = TQ_WINDOW * n_sub
    nq = SEQ // tq
    kern = functools.partial(_window_kernel, layer=layer, tq=TQ_WINDOW, n_sub=n_sub)
    return pl.pallas_call(
        kern,
        grid=(BATCH, nq),
        in_specs=[
            pl.BlockSpec((tq, 4 * LANES), lambda b, i: (b * nq + i, 0)),
            pl.BlockSpec((SEQ, LANES), lambda b, i: (b, 0)),
            pl.BlockSpec((SEQ, LANES), lambda b, i: (b, 1)),
            pl.BlockSpec((CTX_LEN, LANES), lambda b, i: (CTX_BLK0 + b, 0)),
            pl.BlockSpec((CTX_LEN, LANES), lambda b, i: (CTX_BLK0 + b, 1)),
            _SMEM_SPEC,
        ],
        out_specs=pl.BlockSpec((tq, 4 * LANES), lambda b, i: (b * nq + i, 0)),
        out_shape=jax.ShapeDtypeStruct((N_LAT, 4 * LANES), BF16),
        compiler_params=_attn_params(2),
        name="attn_a_window",
    )(q, kvab, kvab, kvab, kvab, sink)


def _latent_dense_split(q, k, v):
    tq = TQ_DENSE_SPLIT
    nq = SEQ // tq
    kern = functools.partial(_dense_split_kernel, tq=tq, n_seg=2)
    return pl.pallas_call(
        kern,
        grid=(BATCH, nq),
        in_specs=[
            pl.BlockSpec((tq, 8 * LANES), lambda b, i: (b * nq + i, 0)),
            pl.BlockSpec((SEQ, 8 * LANES), lambda b, i: (b, 0)),
            pl.BlockSpec((SEQ, 4 * LANES), lambda b, i: (b, 0)),
            pl.BlockSpec((CTX_LEN, 8 * LANES), lambda b, i: (CTX_BLK0 + b, 0)),
            pl.BlockSpec((CTX_LEN, 4 * LANES), lambda b, i: (CTX_BLK0 + b, 0)),
        ],
        out_specs=pl.BlockSpec((tq, 4 * LANES), lambda b, i: (b * nq + i, 0)),
        out_shape=jax.ShapeDtypeStruct((N_LAT, 4 * LANES), BF16),
        compiler_params=_attn_params(2),
        name="attn_c_latent",
    )(q, k, v, k, v)


def _context_shared(q, kvab, k_blk, v_blk, sink, layer, name):
    tq = CTX_LEN
    use_sink = sink is not None
    kern = functools.partial(_dense_shared_kernel, layer=layer, tq=tq, n_seg=1, use_sink=use_sink)
    in_specs = [
        pl.BlockSpec((tq, 4 * LANES), lambda b: (CTX_BLK0 + b, 0)),
        pl.BlockSpec((CTX_LEN, LANES), lambda b: (CTX_BLK0 + b, k_blk)),
        pl.BlockSpec((CTX_LEN, LANES), lambda b: (CTX_BLK0 + b, v_blk)),
    ]
    args = [q, kvab, kvab]
    if use_sink:
        in_specs.append(_SMEM_SPEC)
        args.append(sink)
    return pl.pallas_call(
        kern,
        grid=(BATCH,),
        in_specs=in_specs,
        out_specs=pl.BlockSpec((tq, 4 * LANES), lambda b: (b, 0)),
        out_shape=jax.ShapeDtypeStruct((N_CTX, 4 * LANES), BF16),
        compiler_params=_attn_params(1),
        name=name,
    )(*args)


def _context_split(q, k, v):
    tq = CTX_LEN
    kern = functools.partial(_dense_split_kernel, tq=tq, n_seg=1)
    return pl.pallas_call(
        kern,
        grid=(BATCH,),
        in_specs=[
            pl.BlockSpec((tq, 8 * LANES), lambda b: (CTX_BLK0 + b, 0)),
            pl.BlockSpec((CTX_LEN, 8 * LANES), lambda b: (CTX_BLK0 + b, 0)),
            pl.BlockSpec((CTX_LEN, 4 * LANES), lambda b: (CTX_BLK0 + b, 0)),
        ],
        out_specs=pl.BlockSpec((tq, 4 * LANES), lambda b: (b, 0)),
        out_shape=jax.ShapeDtypeStruct((N_CTX, 4 * LANES), BF16),
        compiler_params=_attn_params(1),
        name="attn_c_context",
    )(q, k, v)


def _merge_kernel(x_ref, sh_ref, sc_ref, gt_ref, g_ref, wg_ref, *refs, split):
    n_y = 6 if split else 3
    y_refs, (wa_ref, wb_ref, wc_ref, wo_ref, o_ref) = refs[:n_y], refs[n_y:]
    x = x_ref[...]
    u = _adaln_bf16(x, g_ref[...], sh_ref[0], sc_ref[0])
    m = None
    for idx, w_ref in enumerate((wa_ref, wb_ref, wc_ref)):
        y = _pick_rows(y_refs[2 * idx], y_refs[2 * idx + 1], TM_MERGE) if split else y_refs[idx][...]
        gate = _dot(u, wg_ref[:, idx * D_MODEL:(idx + 1) * D_MODEL])
        term = jax.nn.sigmoid(gate) * _dot(y, w_ref[...])
        m = term if m is None else m + term
    o_ref[...] = x + gt_ref[0] * _dot(m.astype(BF16), wo_ref[...])


def _merge(x, mods, layer, gain, wg, ys, wa, wb, wc, wo, n_rows):
    tm = TM_MERGE
    split = isinstance(ys[0], tuple)

    def lmat(r, c):
        return pl.BlockSpec((None, r, c), lambda i: (layer, 0, 0))

    if split:
        y_specs = list(_split_row_specs(tm, 4 * LANES)) * 3
        y_args = [a for pair in ys for a in pair]
    else:
        y_specs = [pl.BlockSpec((tm, 4 * LANES), lambda i: (i, 0))] * 3
        y_args = list(ys)
    return pl.pallas_call(
        functools.partial(_merge_kernel, split=split),
        grid=(n_rows // tm,),
        in_specs=[
            pl.BlockSpec((tm, D_MODEL), lambda i: (i, 0)),
            _mod_spec(layer, 3, tm), _mod_spec(layer, 4, tm), _mod_spec(layer, 5, tm),
            pl.BlockSpec((None, 1, D_MODEL), lambda i: (layer, 0, 0)),
            lmat(D_MODEL, GATE_COLS),
            *y_specs,
            lmat(4 * LANES, D_MODEL), lmat(4 * LANES, D_MODEL), lmat(4 * LANES, D_MODEL),
            lmat(D_MODEL, D_MODEL),
        ],
        out_specs=pl.BlockSpec((tm, D_MODEL), lambda i: (i, 0)),
        out_shape=jax.ShapeDtypeStruct((n_rows, D_MODEL), F32),
        compiler_params=pltpu.CompilerParams(
            dimension_semantics=("parallel",), vmem_limit_bytes=VMEM_LIMIT),
        name="merge",
    )(x, mods, mods, mods, gain, wg, *y_args, wa, wb, wc, wo)


def kernel(x, c, ctx, c_ctx, ada_w, ada_b, ffn1_norm, ffn1_w_in, ffn1_w_out, mix_norm, mix_w_in, a_q_norm, a_k_norm, a_sink, b_q_norm, b_k_norm, c_q_lat_norm, c_w_uq, c_kv_lat_norm, c_w_ukv, c_q_norm, c_k_norm, w_branch_a, w_branch_b, w_branch_c, mix_w_out, ffn2_norm, ffn2_w_in, ffn2_w_out):
    n_l = mix_w_in.shape[0]
    wp = _permute_cols(_proj_weight_natural(mix_w_in), PROJ_BLOCK_KINDS)
    wg = mix_w_in[:, :, KV_COLS + Q_COLS:].astype(BF16)
    ukv = c_w_ukv.reshape(n_l, C_KV_RANK, N_HEADS, C_NOPE_DIM + C_V_DIM)
    wkc = _permute_cols(_pad_heads(ukv, C_NOPE_DIM), ("c",) * N_HEADS)
    wvc = ukv[..., C_NOPE_DIM:].reshape(n_l, C_KV_RANK, N_HEADS * C_V_DIM).astype(BF16)
    uq = c_w_uq.reshape(n_l, C_Q_RANK, N_HEADS, C_QK_DIM)
    wqc = _permute_cols(_pad_heads(uq, C_QK_DIM), ("c",) * N_HEADS)

    def pair_rows(w):
        return w.reshape(n_l, 2, 4, HEAD_DIM, D_MODEL).transpose(0, 2, 1, 3, 4).reshape(n_l, 8 * HEAD_DIM, D_MODEL)

    wa = pair_rows(w_branch_a).astype(BF16)
    wb = pair_rows(w_branch_b).astype(BF16)
    wc = w_branch_c.astype(BF16)
    wo = mix_w_out.astype(BF16)
    f1_in, f1_out = ffn1_w_in.astype(BF16), ffn1_w_out.astype(BF16)
    f2_in, f2_out = ffn2_w_in.astype(BF16), ffn2_w_out.astype(BF16)
    gains = (_pair_gain(a_k_norm), _pair_gain(a_q_norm), _pair_gain(b_k_norm), _pair_gain(b_q_norm),
             c_kv_lat_norm[:, None, :], c_q_lat_norm[:, None, :], _c_gain(c_k_norm), _c_gain(c_q_norm))
    tables = _rope_tables(TM_PROJ)
    ffn1_g, mix_g, ffn2_g = ffn1_norm[:, None, :], mix_norm[:, None, :], ffn2_norm[:, None, :]

    cond = jnp.concatenate([c, c_ctx[None, :], jnp.zeros((GROUP_PAD - N_GROUPS, D_MODEL), F32)], axis=0)
    mods = _modulation(cond, ada_w, ada_b).reshape(DEPTH * N_MOD * GROUP_PAD, 1, D_MODEL)

    xs = None
    for l in range(DEPTH):
        last = l == DEPTH - 1
        if l == 0:
            xs = _ffn(x.reshape(N_LAT, D_MODEL), mods, l, 0, ffn1_g, f1_in, f1_out, ctx.reshape(N_CTX, D_MODEL))
        else:
            xs = _ffn(xs, mods, l, 0, ffn1_g, f1_in, f1_out)
        kvab, qa, qb, qc, kc, vc = _project(xs, mods, l, mix_g, wp, wkc, wvc, wqc, gains, tables)
        ys = [_latent_window(qa, kvab, a_sink, l),
              _latent_dense_shared(qb, kvab, 2, 3, l, "attn_b_latent"),
              _latent_dense_split(qc, kc, vc)]
        if not last:
            ys = list(zip(ys, [_context_shared(qa, kvab, 0, 1, a_sink, l, "attn_a_context"),
                               _context_shared(qb, kvab, 2, 3, None, l, "attn_b_context"),
                               _context_split(qc, kc, vc)]))
        xs = _merge(xs, mods, l, mix_g, wg, ys, wa, wb, wc, wo, N_LAT if last else N_ROWS)
        xs = _ffn(xs, mods, l, 6, ffn2_g, f2_in, f2_out)
    return xs.reshape(BATCH, SEQ, D_MODEL)
```

```python
import functools
import math

import numpy as np
import jax
import jax.numpy as jnp
from jax import lax
from jax.experimental import pallas as pl
from jax.experimental.pallas import tpu as pltpu

F32 = jnp.float32
BF16 = jnp.bfloat16

D_MODEL = 1024
BATCH = 8
SEQ = 2048
DEPTH = 2
GRID_W = 64
CTX_LEN = 256
HEAD_DIM = 64
WINDOW = 128
ROPE_THETA = 10000.0
NORM_EPS = 1e-6
N_MOD = 9
D_FF = 2816
N_HEADS = 8
C_NOPE_DIM = 64
C_ROPE_DIM = 32
C_QK_DIM = C_NOPE_DIM + C_ROPE_DIM
C_V_DIM = 64
C_Q_RANK = 384
C_KV_RANK = 256
KV_COLS = 800
Q_COLS = 1408
GATE_COLS = 3 * D_MODEL

LANES = 128
N_LAT = BATCH * SEQ
N_CTX = BATCH * CTX_LEN
N_ROWS = N_LAT + N_CTX
N_GROUPS = BATCH + 1
GROUP_PAD = 16
CTX_BLK0 = N_LAT // CTX_LEN

VMEM_LIMIT = 56 * 1024 * 1024
LOG2E = math.log2(math.e)

PB_AK, PB_AV, PB_BK, PB_BV, PB_CKV, PB_AQ, PB_BQ, PB_CKR, PB_CQ, PB_END = 0, 1, 2, 3, 4, 6, 10, 14, 15, 18
PROJ_COLS = PB_END * LANES

TM_FFN = 512
TM_PROJ = 512
TM_MERGE = 1024
TQ_DENSE_SHARED = 1024
TQ_DENSE_SPLIT = 1024
ATTN_UNIT_ROWS = 512
TQ_WINDOW = 256
WINDOW_SUBTILES = 4

_SMEM_SPEC = pl.BlockSpec(memory_space=pltpu.SMEM)


def _proj_weight_natural(w_in):
    def zeros(n):
        return jnp.zeros(w_in.shape[:-1] + (n,), w_in.dtype)

    q0 = KV_COLS
    blocks = [w_in[..., 0:768]]
    for base in (q0, q0 + 512):
        for j in range(4):
            blocks += [w_in[..., base + j * 64:base + (j + 1) * 64],
                       w_in[..., base + (j + 4) * 64:base + (j + 5) * 64]]
    blocks += [zeros(C_NOPE_DIM), w_in[..., 768:800], zeros(32)]
    blocks.append(w_in[..., q0 + 1024:q0 + 1024 + C_Q_RANK])
    return jnp.concatenate(blocks, axis=-1).astype(BF16)


PROJ_BLOCK_KINDS = (("pair", "id") * 2 + ("id",) * 2 + ("pair",) * 8 + ("c",) + ("id",) * 3)


def _pad_heads(w, n_used):
    lead = w.shape[:-1]
    blk = jnp.concatenate([w[..., :n_used], jnp.zeros(lead + (LANES - n_used,), w.dtype)], axis=-1)
    return blk.reshape(lead[:-1] + (N_HEADS * LANES,)).astype(BF16)


def _perm_matrix(kind):
    r = lax.broadcasted_iota(jnp.int32, (LANES, LANES), 0)
    c = lax.broadcasted_iota(jnp.int32, (LANES, LANES), 1)
    if kind == "pair":
        src = ((c >> 5) & 1) * 64 + 2 * (c & 31) + (c >> 6)
    else:
        src = jnp.where(c < 16, 64 + 2 * c,
                        jnp.where(c < 64, c - 16,
                                  jnp.where(c < 80, 2 * c - 63,
                                            jnp.where(c < 96, c - 32, -1))))
    return jnp.where(r == src, 1.0, 0.0).astype(BF16)


def _permute_kernel(w_ref, o_ref, *, kinds):
    mats = {k: _perm_matrix(k) for k in set(kinds) if k != "id"}
    for j, kind in enumerate(kinds):
        sl = slice(j * LANES, (j + 1) * LANES)
        if kind == "id":
            o_ref[:, sl] = w_ref[:, sl]
        else:
            o_ref[:, sl] = _dot(w_ref[:, sl], mats[kind]).astype(BF16)


def _permute_cols(w, kinds):
    n_l, r, c = w.shape
    spec = pl.BlockSpec((None, r, c), lambda l: (l, 0, 0))
    return pl.pallas_call(
        functools.partial(_permute_kernel, kinds=kinds),
        grid=(n_l,),
        in_specs=[spec],
        out_specs=spec,
        out_shape=jax.ShapeDtypeStruct(w.shape, BF16),
        compiler_params=pltpu.CompilerParams(dimension_semantics=("parallel",), vmem_limit_bytes=VMEM_LIMIT),
        name="permute_cols",
    )(w)


def _pair_gain(g):
    e, o = g[:, 0::2], g[:, 1::2]
    return jnp.stack([jnp.concatenate([e, e, o, o], axis=-1), jnp.concatenate([o, o, e, e], axis=-1)], axis=1)


def _c_gain(g):
    nope, rope = g[:, :C_NOPE_DIM], g[:, C_NOPE_DIM:]
    zeros = jnp.zeros((g.shape[0], 32), g.dtype)
    lo = jnp.concatenate([rope[:, 0::2], nope[:, :48]], axis=-1)
    hi = jnp.concatenate([rope[:, 1::2], nope[:, 48:], zeros], axis=-1)
    return jnp.stack([jnp.concatenate([lo, hi], axis=-1), jnp.concatenate([hi, lo], axis=-1)], axis=1)


def _rope_tables(tm):
    pos = np.arange(SEQ)
    row = (pos // GRID_W).astype(np.float32)
    col = (pos % GRID_W).astype(np.float32)

    def angles(rot_dim):
        n_freq = rot_dim // 4
        inv = jnp.asarray(ROPE_THETA, F32) ** (-jnp.arange(n_freq, dtype=F32) / n_freq)
        return jnp.concatenate([jnp.asarray(row)[:, None] * inv, jnp.asarray(col)[:, None] * inv], axis=-1)

    a64 = angles(HEAD_DIM)
    c, s = jnp.cos(a64), jnp.sin(a64)
    cos_ab = jnp.concatenate([c, c, c, c], axis=-1)
    sin_ab = jnp.concatenate([-s, -s, s, s], axis=-1)
    a32 = angles(C_ROPE_DIM)
    c, s = jnp.cos(a32), jnp.sin(a32)
    one48, zero48 = jnp.ones((SEQ, 48), F32), jnp.zeros((SEQ, 48), F32)
    cos_c = jnp.concatenate([c, one48, c, one48], axis=-1)
    sin_c = jnp.concatenate([-s, zero48, s, zero48], axis=-1)
    ident_c, ident_s = jnp.ones((tm, LANES), F32), jnp.zeros((tm, LANES), F32)
    return (jnp.concatenate([cos_ab, ident_c]), jnp.concatenate([sin_ab, ident_s]),
            jnp.concatenate([cos_c, ident_c]), jnp.concatenate([sin_c, ident_s]))


def _adaln_bf16(x, gain, shift, scale):
    inv = lax.rsqrt(jnp.mean(x * x, axis=-1, keepdims=True) + NORM_EPS)
    return ((x * inv) * (gain * (1.0 + scale)) + shift).astype(BF16)


def _rms(x, gain, n):
    return x * lax.rsqrt(jnp.sum(x * x, axis=-1, keepdims=True) * (1.0 / n) + NORM_EPS) * gain


def _rope_gain_tables(g_ref, cos, sin, scale):
    return cos * (g_ref[0:1, :] * scale), sin * (g_ref[1:2, :] * scale)


def _rotate_bf16(y, gcos, gsin):
    return (y * gcos + pltpu.roll(y, 64, axis=1) * gsin).astype(BF16)


def _group_ones(width, group_of):
    r = lax.broadcasted_iota(jnp.int32, (width, width), 0)
    c = lax.broadcasted_iota(jnp.int32, (width, width), 1)
    return jnp.where(group_of(r) == group_of(c), 1.0, 0.0).astype(BF16)


def _inv_norm_mxu(x, ones_mat, n):
    return lax.rsqrt(_dot((x * x).astype(BF16), ones_mat) + n * NORM_EPS)


def _dot(a, b):
    return jnp.dot(a, b, preferred_element_type=F32)


def _dot_nt(a, b):
    return lax.dot_general(a, b, (((1,), (1,)), ((), ())), preferred_element_type=F32)


def _mod_kernel(cond_ref, w_ref, b_ref, o_ref):
    c = cond_ref[...]
    s = (c * jax.nn.sigmoid(c)).astype(BF16)
    o_ref[...] = _dot(s, w_ref[...].astype(BF16)) + b_ref[...]


def _modulation(cond, ada_w, ada_b):
    return pl.pallas_call(
        _mod_kernel,
        grid=(DEPTH, N_MOD),
        in_specs=[
            pl.BlockSpec((GROUP_PAD, D_MODEL), lambda l, k: (0, 0)),
            pl.BlockSpec((None, D_MODEL, D_MODEL), lambda l, k: (l, 0, k)),
            pl.BlockSpec((None, 1, D_MODEL), lambda l, k: (l, 0, k)),
        ],
        out_specs=pl.BlockSpec((None, None, GROUP_PAD, D_MODEL), lambda l, k: (l, k, 0, 0)),
        out_shape=jax.ShapeDtypeStruct((DEPTH, N_MOD, GROUP_PAD, D_MODEL), F32),
        compiler_params=pltpu.CompilerParams(
            dimension_semantics=("arbitrary", "arbitrary"), vmem_limit_bytes=VMEM_LIMIT),
        name="modulation",
    )(cond, ada_w, ada_b.reshape(DEPTH, 1, N_MOD * D_MODEL))


def _mod_spec(layer, k, tm):
    base = (layer * N_MOD + k) * GROUP_PAD
    return pl.BlockSpec((1, 1, D_MODEL), lambda i: (base + (i * tm) // SEQ, 0, 0))


def _split_row_specs(tm, cols):
    n_lat = N_LAT // tm
    return (pl.BlockSpec((tm, cols), lambda i: (jnp.minimum(i, n_lat - 1), 0)),
            pl.BlockSpec((tm, cols), lambda i: (jnp.maximum(i - n_lat, 0), 0)))


def _pick_rows(lat_ref, ctx_ref, tm):
    return jnp.where(pl.program_id(0) >= N_LAT // tm, ctx_ref[...], lat_ref[...])


def _ffn_kernel(*refs, split):
    if split:
        x = _pick_rows(refs[0], refs[1], TM_FFN)
        refs = refs[1:]
    else:
        x = refs[0][...]
    _, sh_ref, sc_ref, gt_ref, g_ref, wi_ref, wo_ref, o_ref = refs
    h = _adaln_bf16(x, g_ref[...], sh_ref[0], sc_ref[0])
    gate = _dot(h, wi_ref[:, :D_FF])
    up = _dot(h, wi_ref[:, D_FF:])
    act = (gate * jax.nn.sigmoid(gate) * up).astype(BF16)
    o_ref[...] = x + 0.5 * gt_ref[0] * _dot(act, wo_ref[...])


def _ffn(x, mods, layer, k0, gain, w_in, w_out, x_ctx=None):
    tm = TM_FFN
    split = x_ctx is not None
    n_rows = x.shape[0] + (x_ctx.shape[0] if split else 0)
    resident = pl.Buffered(1)
    x_specs = list(_split_row_specs(tm, D_MODEL)) if split else [pl.BlockSpec((tm, D_MODEL), lambda i: (i, 0))]
    return pl.pallas_call(
        functools.partial(_ffn_kernel, split=split),
        grid=(n_rows // tm,),
        in_specs=x_specs + [
            _mod_spec(layer, k0, tm), _mod_spec(layer, k0 + 1, tm), _mod_spec(layer, k0 + 2, tm),
            pl.BlockSpec((None, 1, D_MODEL), lambda i: (layer, 0, 0)),
            pl.BlockSpec((None, D_MODEL, 2 * D_FF), lambda i: (layer, 0, 0), pipeline_mode=resident),
            pl.BlockSpec((None, D_FF, D_MODEL), lambda i: (layer, 0, 0), pipeline_mode=resident),
        ],
        out_specs=pl.BlockSpec((tm, D_MODEL), lambda i: (i, 0)),
        out_shape=jax.ShapeDtypeStruct((n_rows, D_MODEL), F32),
        compiler_params=pltpu.CompilerParams(
            dimension_semantics=("parallel",), vmem_limit_bytes=VMEM_LIMIT),
        name="ffn",
    )(*([x, x_ctx] if split else [x]), mods, mods, mods, gain, w_in, w_out)


ST_KVAB, ST_QA, ST_QB, ST_KC, ST_QC, ST_VC, ST_END = 0, 4, 8, 12, 20, 28, 32


def _proj_kernel(x_ref, sh_ref, sc_ref, g_ref, wp_ref, wkc_ref, wvc_ref, wqc_ref,
                 gak_ref, gaq_ref, gbk_ref, gbq_ref, gkvl_ref, gql_ref, gck_ref, gcq_ref,
                 cab_ref, sab_ref, cc_ref, sc_c_ref,
                 kvab_ref, qa_ref, qb_ref, qc_ref, kc_ref, vc_ref, stage_ref):
    @pl.when(pl.program_id(0) == 0)
    def _():
        stage_ref[...] = jnp.zeros_like(stage_ref)

    def staged(b, n=1):
        return stage_ref[:, b * LANES:(b + n) * LANES]

    def stage(b, val):
        stage_ref[:, b * LANES:b * LANES + val.shape[1]] = val

    cab, sab, cc, sc_c = cab_ref[...], sab_ref[...], cc_ref[...], sc_c_ref[...]
    root_hd, root_c = HEAD_DIM ** 0.5, C_QK_DIM ** 0.5
    for b, g_ref_k in ((0, gak_ref), (2, gbk_ref)):
        gcos, gsin = _rope_gain_tables(g_ref_k, cab, sab, root_hd)
        kvab_ref[:, b * LANES:(b + 1) * LANES] = _rotate_bf16(staged(ST_KVAB + b), gcos, gsin)
        kvab_ref[:, (b + 1) * LANES:(b + 2) * LANES] = staged(ST_KVAB + b + 1).astype(BF16)
    for st, g_ref_q, q_ref in ((ST_QA, gaq_ref, qa_ref), (ST_QB, gbq_ref, qb_ref)):
        gcos, gsin = _rope_gain_tables(g_ref_q, cab, sab, root_hd * HEAD_DIM ** -0.5 * LOG2E)
        for j in range(4):
            q_ref[:, j * LANES:(j + 1) * LANES] = _rotate_bf16(staged(st + j), gcos, gsin)
    gcos_k, gsin_k = _rope_gain_tables(gck_ref, cc, sc_c, root_c)
    gcos_q, gsin_q = _rope_gain_tables(gcq_ref, cc, sc_c, root_c * C_QK_DIM ** -0.5 * LOG2E)
    for hd in range(N_HEADS):
        sl = slice(hd * LANES, (hd + 1) * LANES)
        kc_ref[:, sl] = _rotate_bf16(staged(ST_KC + hd), gcos_k, gsin_k)
        qc_ref[:, sl] = _rotate_bf16(staged(ST_QC + hd), gcos_q, gsin_q)
    vc_ref[...] = staged(ST_VC, 4).astype(BF16)

    h = _adaln_bf16(x_ref[...], g_ref[...], sh_ref[0], sc_ref[0])

    raw = _dot(h, wp_ref[...])

    def project(b, n):
        return raw[:, b * LANES:(b + n) * LANES]

    pair_ones = _group_ones(LANES, lambda l: (l >> 5) & 1)
    pair_ones2 = _group_ones(2 * LANES, lambda l: ((l >> 5) & 1) + 2 * (l >> 7))
    head_ones2 = _group_ones(2 * LANES, lambda l: l >> 7)

    for b in (PB_AK, PB_BK):
        r = project(b, 2)
        k = r[:, :LANES]
        stage(ST_KVAB + b, k * _inv_norm_mxu(k, pair_ones, HEAD_DIM))
        stage(ST_KVAB + b + 1, r[:, LANES:])
    for st, pb in ((ST_QA, PB_AQ), (ST_QB, PB_BQ)):
        for j in (0, 2):
            r = project(pb + j, 2)
            stage(st + j, r * _inv_norm_mxu(r, pair_ones2, HEAD_DIM))

    ckv = _rms(project(PB_CKV, 2), gkvl_ref[...], C_KV_RANK).astype(BF16)
    stage(ST_VC, _dot(ckv, wvc_ref[...]))
    r = project(PB_CKR, 4)
    k_rope2 = jnp.concatenate([r[:, :LANES], r[:, :LANES]], axis=1)
    cq = _rms(r[:, LANES:], gql_ref[...], C_Q_RANK).astype(BF16)
    k_up = _dot(ckv, wkc_ref[...])
    q_up = _dot(cq, wqc_ref[...])
    for hp in range(N_HEADS // 2):
        sl2 = slice(2 * hp * LANES, (2 * hp + 2) * LANES)
        k2 = k_up[:, sl2] + k_rope2
        stage(ST_KC + 2 * hp, k2 * _inv_norm_mxu(k2, head_ones2, C_QK_DIM))
        q2 = q_up[:, sl2]
        stage(ST_QC + 2 * hp, q2 * _inv_norm_mxu(q2, head_ones2, C_QK_DIM))


def _project(x, mods, layer, gain, wp, wkc, wvc, wqc, gains, tables):
    tm = TM_PROJ
    n_tiles = N_ROWS // tm
    n_lat_tiles = N_LAT // tm
    tiles_per_seq = SEQ // tm

    def cur(i):
        return jnp.minimum(i, n_tiles - 1)

    def prev(i):
        return jnp.maximum(i - 1, 0)

    def tab_idx(i):
        t = prev(i)
        return (jnp.where(t < n_lat_tiles, t % tiles_per_seq, tiles_per_seq), 0)

    def mod_spec(k):
        base = (layer * N_MOD + k) * GROUP_PAD
        return pl.BlockSpec((1, 1, D_MODEL), lambda i: (base + (cur(i) * tm) // SEQ, 0, 0))

    def lvec(n):
        return pl.BlockSpec((None, 1, n), lambda i: (layer, 0, 0))

    def lmat(r, c, **kw):
        return pl.BlockSpec((None, r, c), lambda i: (layer, 0, 0), **kw)

    def rows(c):
        return pl.BlockSpec((tm, c), lambda i: (prev(i), 0)), jax.ShapeDtypeStruct((N_ROWS, c), BF16)

    out = [rows(4 * LANES), rows(4 * LANES), rows(4 * LANES), rows(8 * LANES), rows(8 * LANES), rows(4 * LANES)]
    tab_spec = pl.BlockSpec((tm, LANES), tab_idx)
    resident = dict(pipeline_mode=pl.Buffered(1))
    return pl.pallas_call(
        _proj_kernel,
        grid=(n_tiles + 1,),
        in_specs=[
            pl.BlockSpec((tm, D_MODEL), lambda i: (cur(i), 0)),
            mod_spec(3), mod_spec(4),
            lvec(D_MODEL),
            lmat(D_MODEL, PROJ_COLS, **resident), lmat(C_KV_RANK, 8 * LANES, **resident),
            lmat(C_KV_RANK, 4 * LANES, **resident), lmat(C_Q_RANK, 8 * LANES, **resident),
            lmat(2, LANES), lmat(2, LANES), lmat(2, LANES), lmat(2, LANES), lvec(C_KV_RANK), lvec(C_Q_RANK),
            lmat(2, LANES), lmat(2, LANES),
            tab_spec, tab_spec, tab_spec, tab_spec,
        ],
        out_specs=[o[0] for o in out],
        out_shape=[o[1] for o in out],
        scratch_shapes=[pltpu.VMEM((tm, ST_END * LANES), F32)],
        compiler_params=pltpu.CompilerParams(
            dimension_semantics=("arbitrary",), vmem_limit_bytes=VMEM_LIMIT),
        name="project",
    )(x, mods, mods, gain, wp, wkc, wvc, wqc, *gains, *tables)


def _with_ones(v):
    return jnp.concatenate([v, jnp.ones_like(v)], axis=1)


def _attend(q, segs, sink_col):
    scores = []
    for k, _, bias in segs:
        s = _dot_nt(q, k)
        scores.append(s if bias is None else s + bias)
    m = functools.reduce(jnp.maximum, [jnp.max(s, axis=-1, keepdims=True) for s in scores])
    if sink_col is not None:
        m = jnp.maximum(m, sink_col)
    acc = None
    for s, (_, v, _) in zip(scores, segs):
        part = _dot(jnp.exp2(s - m).astype(BF16), v)
        acc = part if acc is None else acc + part
    denom = acc[:, LANES:]
    if sink_col is not None:
        denom = denom + jnp.exp2(sink_col - m)
    return acc[:, :LANES] / denom


def _pair_masks():
    lane = lax.broadcasted_iota(jnp.int32, (1, LANES), 1)
    return (lane & 63) < 32, lane < 64


def _dense_shared_kernel(*refs, layer, tq, n_seg, use_sink):
    q_ref = refs[0]
    kv = refs[1:1 + 2 * n_seg]
    sink_ref = refs[1 + 2 * n_seg] if use_sink else None
    o_ref = refs[-1]
    head0, low = _pair_masks()
    zero = jnp.zeros((), BF16)
    unit = min(tq, ATTN_UNIT_ROWS)
    segs = [(kv[2 * s][...], _with_ones(kv[2 * s + 1][...]), None) for s in range(n_seg)]
    for sub in range(tq // unit):
        rows = slice(sub * unit, (sub + 1) * unit)
        for j in range(4):
            qb = q_ref[rows, j * LANES:(j + 1) * LANES]
            outs = []
            for half in range(2):
                q = jnp.where(head0, qb, zero) if half == 0 else jnp.where(head0, zero, qb)
                sink_col = (jnp.full((unit, 1), sink_ref[layer, j + 4 * half] * LOG2E, F32) if use_sink else None)
                outs.append(_attend(q, segs, sink_col))
            o_ref[rows, j * LANES:(j + 1) * LANES] = jnp.where(low, outs[0], outs[1]).astype(o_ref.dtype)


def _dense_split_kernel(*refs, tq, n_seg):
    q_ref = refs[0]
    kv = refs[1:1 + 2 * n_seg]
    o_ref = refs[-1]
    _, low = _pair_masks()
    unit = min(tq, ATTN_UNIT_ROWS)
    for j in range(4):
        vsl = slice(j * LANES, (j + 1) * LANES)
        vals = [_with_ones(kv[2 * s + 1][:, vsl]) for s in range(n_seg)]
        for sub in range(tq // unit):
            rows = slice(sub * unit, (sub + 1) * unit)
            outs = []
            for half in range(2):
                ksl = slice((2 * j + half) * LANES, (2 * j + half + 1) * LANES)
                segs = [(kv[2 * s][:, ksl], vals[s], None) for s in range(n_seg)]
                outs.append(_attend(q_ref[rows, ksl], segs, None))
            o_ref[rows, vsl] = jnp.where(low, outs[0], outs[1]).astype(o_ref.dtype)


def _window_kernel(q_ref, kl_ref, vl_ref, kc_ref, vc_ref, sink_ref, o_ref, *, layer, tq, n_sub):
    span = tq + 2 * WINDOW
    head0, low = _pair_masks()
    zero = jnp.zeros((), BF16)
    ctx_seg = (kc_ref[...], _with_ones(vc_ref[...]), None)
    for sub in range(n_sub):
        rows = slice(sub * tq, (sub + 1) * tq)
        start = (pl.program_id(1) * n_sub + sub) * tq
        lo = pl.multiple_of(jnp.clip(start - WINDOW, 0, SEQ - span), WINDOW)
        q_pos = start + lax.broadcasted_iota(jnp.int32, (tq, span), 0)
        k_pos = lo + lax.broadcasted_iota(jnp.int32, (tq, span), 1)
        band = jnp.where(jnp.abs(q_pos - k_pos) <= WINDOW, 0.0, -jnp.inf)
        band = jnp.concatenate([band, band], axis=0)
        segs = [(kl_ref[pl.ds(lo, span), :], _with_ones(vl_ref[pl.ds(lo, span), :]), band), ctx_seg]
        for j in range(4):
            qb = q_ref[rows, j * LANES:(j + 1) * LANES]
            q = jnp.concatenate([jnp.where(head0, qb, zero), jnp.where(head0, zero, qb)], axis=0)
            sink_col = jnp.concatenate([jnp.full((tq, 1), sink_ref[layer, j] * LOG2E, F32),
                                        jnp.full((tq, 1), sink_ref[layer, j + 4] * LOG2E, F32)], axis=0)
            o = _attend(q, segs, sink_col)
            o_ref[rows, j * LANES:(j + 1) * LANES] = jnp.where(low, o[:tq], o[tq:]).astype(o_ref.dtype)


def _attn_params(n_grid):
    return pltpu.CompilerParams(dimension_semantics=("parallel",) * n_grid, vmem_limit_bytes=VMEM_LIMIT)


def _latent_dense_shared(q, kvab, k_blk, v_blk, layer, name):
    tq = TQ_DENSE_SHARED
    nq = SEQ // tq
    kern = functools.partial(_dense_shared_kernel, layer=layer, tq=tq, n_seg=2, use_sink=False)
    return pl.pallas_call(
        kern,
        grid=(BATCH, nq),
        in_specs=[
            pl.BlockSpec((tq, 4 * LANES), lambda b, i: (b * nq + i, 0)),
            pl.BlockSpec((SEQ, LANES), lambda b, i: (b, k_blk)),
            pl.BlockSpec((SEQ, LANES), lambda b, i: (b, v_blk)),
            pl.BlockSpec((CTX_LEN, LANES), lambda b, i: (CTX_BLK0 + b, k_blk)),
            pl.BlockSpec((CTX_LEN, LANES), lambda b, i: (CTX_BLK0 + b, v_blk)),
        ],
        out_specs=pl.BlockSpec((tq, 4 * LANES), lambda b, i: (b * nq + i, 0)),
        out_shape=jax.ShapeDtypeStruct((N_LAT, 4 * LANES), BF16),
        compiler_params=_attn_params(2),
        name=name,
    )(q, kvab, kvab, kvab, kvab)


def _latent_window(q, kvab, sink, layer):
    n_sub = WINDOW_SUBTILES
    tq = TQ_WINDOW * n_sub
    nq = SEQ // tq
    kern = functools.partial(_window_kernel, layer=layer, tq=TQ_WINDOW, n_sub=n_sub)
    return pl.pallas_call(
        kern,
        grid=(BATCH, nq),
        in_specs=[
            pl.BlockSpec((tq, 4 * LANES), lambda b, i: (b * nq + i, 0)),
            pl.BlockSpec((SEQ, LANES), lambda b, i: (b, 0)),
            pl.BlockSpec((SEQ, LANES), lambda b, i: (b, 1)),
            pl.BlockSpec((CTX_LEN, LANES), lambda b, i: (CTX_BLK0 + b, 0)),
            pl.BlockSpec((CTX_LEN, LANES), lambda b, i: (CTX_BLK0 + b, 1)),
            _SMEM_SPEC,
        ],
        out_specs=pl.BlockSpec((tq, 4 * LANES), lambda b, i: (b * nq + i, 0)),
        out_shape=jax.ShapeDtypeStruct((N_LAT, 4 * LANES), BF16),
        compiler_params=_attn_params(2),
        name="attn_a_window",
    )(q, kvab, kvab, kvab, kvab, sink)


def _latent_dense_split(q, k, v):
    tq = TQ_DENSE_SPLIT
    nq = SEQ // tq
    kern = functools.partial(_dense_split_kernel, tq=tq, n_seg=2)
    return pl.pallas_call(
        kern,
        grid=(BATCH, nq),
        in_specs=[
            pl.BlockSpec((tq, 8 * LANES), lambda b, i: (b * nq + i, 0)),
            pl.BlockSpec((SEQ, 8 * LANES), lambda b, i: (b, 0)),
            pl.BlockSpec((SEQ, 4 * LANES), lambda b, i: (b, 0)),
            pl.BlockSpec((CTX_LEN, 8 * LANES), lambda b, i: (CTX_BLK0 + b, 0)),
            pl.BlockSpec((CTX_LEN, 4 * LANES), lambda b, i: (CTX_BLK0 + b, 0)),
        ],
        out_specs=pl.BlockSpec((tq, 4 * LANES), lambda b, i: (b * nq + i, 0)),
        out_shape=jax.ShapeDtypeStruct((N_LAT, 4 * LANES), BF16),
        compiler_params=_attn_params(2),
        name="attn_c_latent",
    )(q, k, v, k, v)


def _context_shared(q, kvab, k_blk, v_blk, sink, layer, name):
    tq = CTX_LEN
    use_sink = sink is not None
    kern = functools.partial(_dense_shared_kernel, layer=layer, tq=tq, n_seg=1, use_sink=use_sink)
    in_specs = [
        pl.BlockSpec((tq, 4 * LANES), lambda b: (CTX_BLK0 + b, 0)),
        pl.BlockSpec((CTX_LEN, LANES), lambda b: (CTX_BLK0 + b, k_blk)),
        pl.BlockSpec((CTX_LEN, LANES), lambda b: (CTX_BLK0 + b, v_blk)),
    ]
    args = [q, kvab, kvab]
    if use_sink:
        in_specs.append(_SMEM_SPEC)
        args.append(sink)
    return pl.pallas_call(
        kern,
        grid=(BATCH,),
        in_specs=in_specs,
        out_specs=pl.BlockSpec((tq, 4 * LANES), lambda b: (b, 0)),
        out_shape=jax.ShapeDtypeStruct((N_CTX, 4 * LANES), BF16),
        compiler_params=_attn_params(1),
        name=name,
    )(*args)


def _context_split(q, k, v):
    tq = CTX_LEN
    kern = functools.partial(_dense_split_kernel, tq=tq, n_seg=1)
    return pl.pallas_call(
        kern,
        grid=(BATCH,),
        in_specs=[
            pl.BlockSpec((tq, 8 * LANES), lambda b: (CTX_BLK0 + b, 0)),
            pl.BlockSpec((CTX_LEN, 8 * LANES), lambda b: (CTX_BLK0 + b, 0)),
            pl.BlockSpec((CTX_LEN, 4 * LANES), lambda b: (CTX_BLK0 + b, 0)),
        ],
        out_specs=pl.BlockSpec((tq, 4 * LANES), lambda b: (b, 0)),
        out_shape=jax.ShapeDtypeStruct((N_CTX, 4 * LANES), BF16),
        compiler_params=_attn_params(1),
        name="attn_c_context",
    )(q, k, v)


def _merge_kernel(x_ref, sh_ref, sc_ref, gt_ref, g_ref, wg_ref, *refs, split):
    n_y = 6 if split else 3
    y_refs, (wa_ref, wb_ref, wc_ref, wo_ref, o_ref) = refs[:n_y], refs[n_y:]
    x = x_ref[...]
    u = _adaln_bf16(x, g_ref[...], sh_ref[0], sc_ref[0])
    m = None
    for idx, w_ref in enumerate((wa_ref, wb_ref, wc_ref)):
        y = _pick_rows(y_refs[2 * idx], y_refs[2 * idx + 1], TM_MERGE) if split else y_refs[idx][...]
        gate = _dot(u, wg_ref[:, idx * D_MODEL:(idx + 1) * D_MODEL])
        term = jax.nn.sigmoid(gate) * _dot(y, w_ref[...])
        m = term if m is None else m + term
    o_ref[...] = x + gt_ref[0] * _dot(m.astype(BF16), wo_ref[...])


def _merge(x, mods, layer, gain, wg, ys, wa, wb, wc, wo, n_rows):
    tm = TM_MERGE
    split = isinstance(ys[0], tuple)

    def lmat(r, c):
        return pl.BlockSpec((None, r, c), lambda i: (layer, 0, 0))

    if split:
        y_specs = list(_split_row_specs(tm, 4 * LANES)) * 3
        y_args = [a for pair in ys for a in pair]
    else:
        y_specs = [pl.BlockSpec((tm, 4 * LANES), lambda i: (i, 0))] * 3
        y_args = list(ys)
    return pl.pallas_call(
        functools.partial(_merge_kernel, split=split),
        grid=(n_rows // tm,),
        in_specs=[
            pl.BlockSpec((tm, D_MODEL), lambda i: (i, 0)),
            _mod_spec(layer, 3, tm), _mod_spec(layer, 4, tm), _mod_spec(layer, 5, tm),
            pl.BlockSpec((None, 1, D_MODEL), lambda i: (layer, 0, 0)),
            lmat(D_MODEL, GATE_COLS),
            *y_specs,
            lmat(4 * LANES, D_MODEL), lmat(4 * LANES, D_MODEL), lmat(4 * LANES, D_MODEL),
            lmat(D_MODEL, D_MODEL),
        ],
        out_specs=pl.BlockSpec((tm, D_MODEL), lambda i: (i, 0)),
        out_shape=jax.ShapeDtypeStruct((n_rows, D_MODEL), F32),
        compiler_params=pltpu.CompilerParams(
            dimension_semantics=("parallel",), vmem_limit_bytes=VMEM_LIMIT),
        name="merge",
    )(x, mods, mods, mods, gain, wg, *y_args, wa, wb, wc, wo)


def kernel(x, c, ctx, c_ctx, ada_w, ada_b, ffn1_norm, ffn1_w_in, ffn1_w_out, mix_norm, mix_w_in, a_q_norm, a_k_norm, a_sink, b_q_norm, b_k_norm, c_q_lat_norm, c_w_uq, c_kv_lat_norm, c_w_ukv, c_q_norm, c_k_norm, w_branch_a, w_branch_b, w_branch_c, mix_w_out, ffn2_norm, ffn2_w_in, ffn2_w_out):
    n_l = mix_w_in.shape[0]
    wp = _permute_cols(_proj_weight_natural(mix_w_in), PROJ_BLOCK_KINDS)
    wg = mix_w_in[:, :, KV_COLS + Q_COLS:].astype(BF16)
    ukv = c_w_ukv.reshape(n_l, C_KV_RANK, N_HEADS, C_NOPE_DIM + C_V_DIM)
    wkc = _permute_cols(_pad_heads(ukv, C_NOPE_DIM), ("c",) * N_HEADS)
    wvc = ukv[..., C_NOPE_DIM:].reshape(n_l, C_KV_RANK, N_HEADS * C_V_DIM).astype(BF16)
    uq = c_w_uq.reshape(n_l, C_Q_RANK, N_HEADS, C_QK_DIM)
    wqc = _permute_cols(_pad_heads(uq, C_QK_DIM), ("c",) * N_HEADS)

    def pair_rows(w):
        return w.reshape(n_l, 2, 4, HEAD_DIM, D_MODEL).transpose(0, 2, 1, 3, 4).reshape(n_l, 8 * HEAD_DIM, D_MODEL)

    wa = pair_rows(w_branch_a).astype(BF16)
    wb = pair_rows(w_branch_b).astype(BF16)
    wc = w_branch_c.astype(BF16)
    wo = mix_w_out.astype(BF16)
    f1_in, f1_out = ffn1_w_in.astype(BF16), ffn1_w_out.astype(BF16)
    f2_in, f2_out = ffn2_w_in.astype(BF16), ffn2_w_out.astype(BF16)
    gains = (_pair_gain(a_k_norm), _pair_gain(a_q_norm), _pair_gain(b_k_norm), _pair_gain(b_q_norm),
             c_kv_lat_norm[:, None, :], c_q_lat_norm[:, None, :], _c_gain(c_k_norm), _c_gain(c_q_norm))
    tables = _rope_tables(TM_PROJ)
    ffn1_g, mix_g, ffn2_g = ffn1_norm[:, None, :], mix_norm[:, None, :], ffn2_norm[:, None, :]

    cond = jnp.concatenate([c, c_ctx[None, :], jnp.zeros((GROUP_PAD - N_GROUPS, D_MODEL), F32)], axis=0)
    mods = _modulation(cond, ada_w, ada_b).reshape(DEPTH * N_MOD * GROUP_PAD, 1, D_MODEL)

    xs = None
    for l in range(DEPTH):
        last = l == DEPTH - 1
        if l == 0:
            xs = _ffn(x.reshape(N_LAT, D_MODEL), mods, l, 0, ffn1_g, f1_in, f1_out, ctx.reshape(N_CTX, D_MODEL))
        else:
            xs = _ffn(xs, mods, l, 0, ffn1_g, f1_in, f1_out)
        kvab, qa, qb, qc, kc, vc = _project(xs, mods, l, mix_g, wp, wkc, wvc, wqc, gains, tables)
        ys = [_latent_window(qa, kvab, a_sink, l),
              _latent_dense_shared(qb, kvab, 2, 3, l, "attn_b_latent"),
              _latent_dense_split(qc, kc, vc)]
        if not last:
            ys = list(zip(ys, [_context_shared(qa, kvab, 0, 1, a_sink, l, "attn_a_context"),
                               _context_shared(qb, kvab, 2, 3, None, l, "attn_b_context"),
                               _context_split(qc, kc, vc)]))
        xs = _merge(xs, mods, l, mix_g, wg, ys, wa, wb, wc, wo, N_LAT if last else N_ROWS)
        xs = _ffn(xs, mods, l, 6, ffn2_g, f2_in, f2_out)
    return xs.reshape(BATCH, SEQ, D_MODEL)
```

```python
import functools
import math

import numpy as np
import jax
import jax.numpy as jnp
from jax import lax
from jax.experimental import pallas as pl
from jax.experimental.pallas import tpu as pltpu

F32 = jnp.float32
BF16 = jnp.bfloat16

D_MODEL = 1024
BATCH = 8
SEQ = 2048
DEPTH = 2
GRID_W = 64
CTX_LEN = 256
HEAD_DIM = 64
WINDOW = 128
ROPE_THETA = 10000.0
NORM_EPS = 1e-6
N_MOD = 9
D_FF = 2816
N_HEADS = 8
C_NOPE_DIM = 64
C_ROPE_DIM = 32
C_QK_DIM = C_NOPE_DIM + C_ROPE_DIM
C_V_DIM = 64
C_Q_RANK = 384
C_KV_RANK = 256
KV_COLS = 800
Q_COLS = 1408
GATE_COLS = 3 * D_MODEL

LANES = 128
N_LAT = BATCH * SEQ
N_CTX = BATCH * CTX_LEN
N_ROWS = N_LAT + N_CTX
N_GROUPS = BATCH + 1
GROUP_PAD = 16
CTX_BLK0 = N_LAT // CTX_LEN

VMEM_LIMIT = 56 * 1024 * 1024
LOG2E = math.log2(math.e)

PB_AK, PB_AV, PB_BK, PB_BV, PB_CKV, PB_AQ, PB_BQ, PB_CKR, PB_CQ, PB_END = 0, 1, 2, 3, 4, 6, 10, 14, 15, 18
PROJ_COLS = PB_END * LANES

TM_FFN = 512
TM_PROJ = 512
TM_MERGE = 1024
TQ_DENSE_SHARED = 1024
TQ_DENSE_SPLIT = 1024
ATTN_UNIT_ROWS = 512
TQ_WINDOW = 256
WINDOW_SUBTILES = 4

_SMEM_SPEC = pl.BlockSpec(memory_space=pltpu.SMEM)


def _proj_weight_natural(w_in):
    def zeros(n):
        return jnp.zeros(w_in.shape[:-1] + (n,), w_in.dtype)

    q0 = KV_COLS
    blocks = [w_in[..., 0:768]]
    for base in (q0, q0 + 512):
        for j in range(4):
            blocks += [w_in[..., base + j * 64:base + (j + 1) * 64],
                       w_in[..., base + (j + 4) * 64:base + (j + 5) * 64]]
    blocks += [zeros(C_NOPE_DIM), w_in[..., 768:800], zeros(32)]
    blocks.append(w_in[..., q0 + 1024:q0 + 1024 + C_Q_RANK])
    return jnp.concatenate(blocks, axis=-1).astype(BF16)


PROJ_BLOCK_KINDS = (("pair", "id") * 2 + ("id",) * 2 + ("pair",) * 8 + ("c",) + ("id",) * 3)


def _pad_heads(w, n_used):
    lead = w.shape[:-1]
    blk = jnp.concatenate([w[..., :n_used], jnp.zeros(lead + (LANES - n_used,), w.dtype)], axis=-1)
    return blk.reshape(lead[:-1] + (N_HEADS * LANES,)).astype(BF16)


def _perm_matrix(kind):
    r = lax.broadcasted_iota(jnp.int32, (LANES, LANES), 0)
    c = lax.broadcasted_iota(jnp.int32, (LANES, LANES), 1)
    if kind == "pair":
        src = ((c >> 5) & 1) * 64 + 2 * (c & 31) + (c >> 6)
    else:
        src = jnp.where(c < 16, 64 + 2 * c,
                        jnp.where(c < 64, c - 16,
                                  jnp.where(c < 80, 2 * c - 63,
                                            jnp.where(c < 96, c - 32, -1))))
    return jnp.where(r == src, 1.0, 0.0).astype(BF16)


def _permute_kernel(w_ref, o_ref, *, kinds):
    mats = {k: _perm_matrix(k) for k in set(kinds) if k != "id"}
    for j, kind in enumerate(kinds):
        sl = slice(j * LANES, (j + 1) * LANES)
        if kind == "id":
            o_ref[:, sl] = w_ref[:, sl]
        else:
            o_ref[:, sl] = _dot(w_ref[:, sl], mats[kind]).astype(BF16)


def _permute_cols(w, kinds):
    n_l, r, c = w.shape
    spec = pl.BlockSpec((None, r, c), lambda l: (l, 0, 0))
    return pl.pallas_call(
        functools.partial(_permute_kernel, kinds=kinds),
        grid=(n_l,),
        in_specs=[spec],
        out_specs=spec,
        out_shape=jax.ShapeDtypeStruct(w.shape, BF16),
        compiler_params=pltpu.CompilerParams(dimension_semantics=("parallel",), vmem_limit_bytes=VMEM_LIMIT),
        name="permute_cols",
    )(w)


def _pair_gain(g):
    e, o = g[:, 0::2], g[:, 1::2]
    return jnp.stack([jnp.concatenate([e, e, o, o], axis=-1), jnp.concatenate([o, o, e, e], axis=-1)], axis=1)


def _c_gain(g):
    nope, rope = g[:, :C_NOPE_DIM], g[:, C_NOPE_DIM:]
    zeros = jnp.zeros((g.shape[0], 32), g.dtype)
    lo = jnp.concatenate([rope[:, 0::2], nope[:, :48]], axis=-1)
    hi = jnp.concatenate([rope[:, 1::2], nope[:, 48:], zeros], axis=-1)
    return jnp.stack([jnp.concatenate([lo, hi], axis=-1), jnp.concatenate([hi, lo], axis=-1)], axis=1)


def _rope_tables(tm):
    pos = np.arange(SEQ)
    row = (pos // GRID_W).astype(np.float32)
    col = (pos % GRID_W).astype(np.float32)

    def angles(rot_dim):
        n_freq = rot_dim // 4
        inv = jnp.asarray(ROPE_THETA, F32) ** (-jnp.arange(n_freq, dtype=F32) / n_freq)
        return jnp.concatenate([jnp.asarray(row)[:, None] * inv, jnp.asarray(col)[:, None] * inv], axis=-1)

    a64 = angles(HEAD_DIM)
    c, s = jnp.cos(a64), jnp.sin(a64)
    cos_ab = jnp.concatenate([c, c, c, c], axis=-1)
    sin_ab = jnp.concatenate([-s, -s, s, s], axis=-1)
    a32 = angles(C_ROPE_DIM)
    c, s = jnp.cos(a32), jnp.sin(a32)
    one48, zero48 = jnp.ones((SEQ, 48), F32), jnp.zeros((SEQ, 48), F32)
    cos_c = jnp.concatenate([c, one48, c, one48], axis=-1)
    sin_c = jnp.concatenate([-s, zero48, s, zero48], axis=-1)
    ident_c, ident_s = jnp.ones((tm, LANES), F32), jnp.zeros((tm, LANES), F32)
    return (jnp.concatenate([cos_ab, ident_c]), jnp.concatenate([sin_ab, ident_s]),
            jnp.concatenate([cos_c, ident_c]), jnp.concatenate([sin_c, ident_s]))


def _adaln_bf16(x, gain, shift, scale):
    inv = lax.rsqrt(jnp.mean(x * x, axis=-1, keepdims=True) + NORM_EPS)
    return ((x * inv) * (gain * (1.0 + scale)) + shift).astype(BF16)


def _rms(x, gain, n):
    return x * lax.rsqrt(jnp.sum(x * x, axis=-1, keepdims=True) * (1.0 / n) + NORM_EPS) * gain


def _rope_gain_tables(g_ref, cos, sin, scale):
    return cos * (g_ref[0:1, :] * scale), sin * (g_ref[1:2, :] * scale)


def _rotate_bf16(y, gcos, gsin):
    return (y * gcos + pltpu.roll(y, 64, axis=1) * gsin).astype(BF16)


def _group_ones(width, group_of):
    r = lax.broadcasted_iota(jnp.int32, (width, width), 0)
    c = lax.broadcasted_iota(jnp.int32, (width, width), 1)
    return jnp.where(group_of(r) == group_of(c), 1.0, 0.0).astype(BF16)


def _inv_norm_mxu(x, ones_mat, n):
    return lax.rsqrt(_dot((x * x).astype(BF16), ones_mat) + n * NORM_EPS)


def _dot(a, b):
    return jnp.dot(a, b, preferred_element_type=F32)


def _dot_nt(a, b):
    return lax.dot_general(a, b, (((1,), (1,)), ((), ())), preferred_element_type=F32)


def _mod_kernel(cond_ref, w_ref, b_ref, o_ref):
    c = cond_ref[...]
    s = (c * jax.nn.sigmoid(c)).astype(BF16)
    o_ref[...] = _dot(s, w_ref[...].astype(BF16)) + b_ref[...]


def _modulation(cond, ada_w, ada_b):
    return pl.pallas_call(
        _mod_kernel,
        grid=(DEPTH, N_MOD),
        in_specs=[
            pl.BlockSpec((GROUP_PAD, D_MODEL), lambda l, k: (0, 0)),
            pl.BlockSpec((None, D_MODEL, D_MODEL), lambda l, k: (l, 0, k)),
            pl.BlockSpec((None, 1, D_MODEL), lambda l, k: (l, 0, k)),
        ],
        out_specs=pl.BlockSpec((None, None, GROUP_PAD, D_MODEL), lambda l, k: (l, k, 0, 0)),
        out_shape=jax.ShapeDtypeStruct((DEPTH, N_MOD, GROUP_PAD, D_MODEL), F32),
        compiler_params=pltpu.CompilerParams(
            dimension_semantics=("arbitrary", "arbitrary"), vmem_limit_bytes=VMEM_LIMIT),
        name="modulation",
    )(cond, ada_w, ada_b.reshape(DEPTH, 1, N_MOD * D_MODEL))


def _mod_spec(layer, k, tm):
    base = (layer * N_MOD + k) * GROUP_PAD
    return pl.BlockSpec((1, 1, D_MODEL), lambda i: (base + (i * tm) // SEQ, 0, 0))


def _split_row_specs(tm, cols):
    n_lat = N_LAT // tm
    return (pl.BlockSpec((tm, cols), lambda i: (jnp.minimum(i, n_lat - 1), 0)),
            pl.BlockSpec((tm, cols), lambda i: (jnp.maximum(i - n_lat, 0), 0)))


def _pick_rows(lat_ref, ctx_ref, tm):
    return jnp.where(pl.program_id(0) >= N_LAT // tm, ctx_ref[...], lat_ref[...])


def _ffn_kernel(*refs, split):
    if split:
        x = _pick_rows(refs[0], refs[1], TM_FFN)
        refs = refs[1:]
    else:
        x = refs[0][...]
    _, sh_ref, sc_ref, gt_ref, g_ref, wi_ref, wo_ref, o_ref = refs
    h = _adaln_bf16(x, g_ref[...], sh_ref[0], sc_ref[0])
    gate = _dot(h, wi_ref[:, :D_FF])
    up = _dot(h, wi_ref[:, D_FF:])
    act = (gate * jax.nn.sigmoid(gate) * up).astype(BF16)
    o_ref[...] = x + 0.5 * gt_ref[0] * _dot(act, wo_ref[...])


def _ffn(x, mods, layer, k0, gain, w_in, w_out, x_ctx=None):
    tm = TM_FFN
    split = x_ctx is not None
    n_rows = x.shape[0] + (x_ctx.shape[0] if split else 0)
    resident = pl.Buffered(1)
    x_specs = list(_split_row_specs(tm, D_MODEL)) if split else [pl.BlockSpec((tm, D_MODEL), lambda i: (i, 0))]
    return pl.pallas_call(
        functools.partial(_ffn_kernel, split=split),
        grid=(n_rows // tm,),
        in_specs=x_specs + [
            _mod_spec(layer, k0, tm), _mod_spec(layer, k0 + 1, tm), _mod_spec(layer, k0 + 2, tm),
            pl.BlockSpec((None, 1, D_MODEL), lambda i: (layer, 0, 0)),
            pl.BlockSpec((None, D_MODEL, 2 * D_FF), lambda i: (layer, 0, 0), pipeline_mode=resident),
            pl.BlockSpec((None, D_FF, D_MODEL), lambda i: (layer, 0, 0), pipeline_mode=resident),
        ],
        out_specs=pl.BlockSpec((tm, D_MODEL), lambda i: (i, 0)),
        out_shape=jax.ShapeDtypeStruct((n_rows, D_MODEL), F32),
        compiler_params=pltpu.CompilerParams(
            dimension_semantics=("parallel",), vmem_limit_bytes=VMEM_LIMIT),
        name="ffn",
    )(*([x, x_ctx] if split else [x]), mods, mods, mods, gain, w_in, w_out)


ST_KVAB, ST_QA, ST_QB, ST_KC, ST_QC, ST_VC, ST_END = 0, 4, 8, 12, 20, 28, 32


def _proj_kernel(x_ref, sh_ref, sc_ref, g_ref, wp_ref, wkc_ref, wvc_ref, wqc_ref,
                 gak_ref, gaq_ref, gbk_ref, gbq_ref, gkvl_ref, gql_ref, gck_ref, gcq_ref,
                 cab_ref, sab_ref, cc_ref, sc_c_ref,
                 kvab_ref, qa_ref, qb_ref, qc_ref, kc_ref, vc_ref, stage_ref):
    @pl.when(pl.program_id(0) == 0)
    def _():
        stage_ref[...] = jnp.zeros_like(stage_ref)

    def staged(b, n=1):
        return stage_ref[:, b * LANES:(b + n) * LANES]

    def stage(b, val):
        stage_ref[:, b * LANES:b * LANES + val.shape[1]] = val

    cab, sab, cc, sc_c = cab_ref[...], sab_ref[...], cc_ref[...], sc_c_ref[...]
    root_hd, root_c = HEAD_DIM ** 0.5, C_QK_DIM ** 0.5
    for b, g_ref_k in ((0, gak_ref), (2, gbk_ref)):
        gcos, gsin = _rope_gain_tables(g_ref_k, cab, sab, root_hd)
        kvab_ref[:, b * LANES:(b + 1) * LANES] = _rotate_bf16(staged(ST_KVAB + b), gcos, gsin)
        kvab_ref[:, (b + 1) * LANES:(b + 2) * LANES] = staged(ST_KVAB + b + 1).astype(BF16)
    for st, g_ref_q, q_ref in ((ST_QA, gaq_ref, qa_ref), (ST_QB, gbq_ref, qb_ref)):
        gcos, gsin = _rope_gain_tables(g_ref_q, cab, sab, root_hd * HEAD_DIM ** -0.5 * LOG2E)
        for j in range(4):
            q_ref[:, j * LANES:(j + 1) * LANES] = _rotate_bf16(staged(st + j), gcos, gsin)
    gcos_k, gsin_k = _rope_gain_tables(gck_ref, cc, sc_c, root_c)
    gcos_q, gsin_q = _rope_gain_tables(gcq_ref, cc, sc_c, root_c * C_QK_DIM ** -0.5 * LOG2E)
    for hd in range(N_HEADS):
        sl = slice(hd * LANES, (hd + 1) * LANES)
        kc_ref[:, sl] = _rotate_bf16(staged(ST_KC + hd), gcos_k, gsin_k)
        qc_ref[:, sl] = _rotate_bf16(staged(ST_QC + hd), gcos_q, gsin_q)
    vc_ref[...] = staged(ST_VC, 4).astype(BF16)

    h = _adaln_bf16(x_ref[...], g_ref[...], sh_ref[0], sc_ref[0])

    raw = _dot(h, wp_ref[...])

    def project(b, n):
        return raw[:, b * LANES:(b + n) * LANES]

    pair_ones = _group_ones(LANES, lambda l: (l >> 5) & 1)
    pair_ones2 = _group_ones(2 * LANES, lambda l: ((l >> 5) & 1) + 2 * (l >> 7))

    for b in (PB_AK, PB_BK):
        r = project(b, 2)
        k = r[:, :LANES]
        stage(ST_KVAB + b, k * _inv_norm_mxu(k, pair_ones, HEAD_DIM))
        stage(ST_KVAB + b + 1, r[:, LANES:])
    for st, pb in ((ST_QA, PB_AQ), (ST_QB, PB_BQ)):
        for j in (0, 2):
            r = project(pb + j, 2)
            stage(st + j, r * _inv_norm_mxu(r, pair_ones2, HEAD_DIM))

    ckv = _rms(project(PB_CKV, 2), gkvl_ref[...], C_KV_RANK).astype(BF16)
    stage(ST_VC, _dot(ckv, wvc_ref[...]))
    r = project(PB_CKR, 4)
    cq = _rms(r[:, LANES:], gql_ref[...], C_Q_RANK).astype(BF16)
    k_up = _dot(ckv, wkc_ref[...])
    q_up = _dot(cq, wqc_ref[...])
    k_rope = r[:, :LANES]

    def unit_norm(v):
        return v * lax.rsqrt(jnp.sum(v * v, axis=-1, keepdims=True) + C_QK_DIM * NORM_EPS)

    for hd in range(N_HEADS):
        sl = slice(hd * LANES, (hd + 1) * LANES)
        stage(ST_KC + hd, unit_norm(k_up[:, sl] + k_rope))
        stage(ST_QC + hd, unit_norm(q_up[:, sl]))


def _project(x, mods, layer, gain, wp, wkc, wvc, wqc, gains, tables):
    tm = TM_PROJ
    n_tiles = N_ROWS // tm
    n_lat_tiles = N_LAT // tm
    tiles_per_seq = SEQ // tm

    def cur(i):
        return jnp.minimum(i, n_tiles - 1)

    def prev(i):
        return jnp.maximum(i - 1, 0)

    def tab_idx(i):
        t = prev(i)
        return (jnp.where(t < n_lat_tiles, t % tiles_per_seq, tiles_per_seq), 0)

    def mod_spec(k):
        base = (layer * N_MOD + k) * GROUP_PAD
        return pl.BlockSpec((1, 1, D_MODEL), lambda i: (base + (cur(i) * tm) // SEQ, 0, 0))

    def lvec(n):
        return pl.BlockSpec((None, 1, n), lambda i: (layer, 0, 0))

    def lmat(r, c, **kw):
        return pl.BlockSpec((None, r, c), lambda i: (layer, 0, 0), **kw)

    def rows(c):
        return pl.BlockSpec((tm, c), lambda i: (prev(i), 0)), jax.ShapeDtypeStruct((N_ROWS, c), BF16)

    out = [rows(4 * LANES), rows(4 * LANES), rows(4 * LANES), rows(8 * LANES), rows(8 * LANES), rows(4 * LANES)]
    tab_spec = pl.BlockSpec((tm, LANES), tab_idx)
    resident = dict(pipeline_mode=pl.Buffered(1))
    return pl.pallas_call(
        _proj_kernel,
        grid=(n_tiles + 1,),
        in_specs=[
            pl.BlockSpec((tm, D_MODEL), lambda i: (cur(i), 0)),
            mod_spec(3), mod_spec(4),
            lvec(D_MODEL),
            lmat(D_MODEL, PROJ_COLS, **resident), lmat(C_KV_RANK, 8 * LANES, **resident),
            lmat(C_KV_RANK, 4 * LANES, **resident), lmat(C_Q_RANK, 8 * LANES, **resident),
            lmat(2, LANES), lmat(2, LANES), lmat(2, LANES), lmat(2, LANES), lvec(C_KV_RANK), lvec(C_Q_RANK),
            lmat(2, LANES), lmat(2, LANES),
            tab_spec, tab_spec, tab_spec, tab_spec,
        ],
        out_specs=[o[0] for o in out],
        out_shape=[o[1] for o in out],
        scratch_shapes=[pltpu.VMEM((tm, ST_END * LANES), F32)],
        compiler_params=pltpu.CompilerParams(
            dimension_semantics=("arbitrary",), vmem_limit_bytes=VMEM_LIMIT),
        name="project",
    )(x, mods, mods, gain, wp, wkc, wvc, wqc, *gains, *tables)


def _with_ones(v):
    return jnp.concatenate([v, jnp.ones_like(v)], axis=1)


def _attend(q, segs, sink_col):
    scores = []
    for k, _, bias in segs:
        s = _dot_nt(q, k)
        scores.append(s if bias is None else s + bias)
    m = functools.reduce(jnp.maximum, [jnp.max(s, axis=-1, keepdims=True) for s in scores])
    if sink_col is not None:
        m = jnp.maximum(m, sink_col)
    acc = None
    for s, (_, v, _) in zip(scores, segs):
        part = _dot(jnp.exp2(s - m).astype(BF16), v)
        acc = part if acc is None else acc + part
    denom = acc[:, LANES:]
    if sink_col is not None:
        denom = denom + jnp.exp2(sink_col - m)
    return acc[:, :LANES] / denom


def _pair_masks():
    lane = lax.broadcasted_iota(jnp.int32, (1, LANES), 1)
    return (lane & 63) < 32, lane < 64


def _dense_shared_kernel(*refs, layer, tq, n_seg, use_sink):
    q_ref = refs[0]
    kv = refs[1:1 + 2 * n_seg]
    sink_ref = refs[1 + 2 * n_seg] if use_sink else None
    o_ref = refs[-1]
    head0, low = _pair_masks()
    zero = jnp.zeros((), BF16)
    unit = min(tq, ATTN_UNIT_ROWS)
    segs = [(kv[2 * s][...], _with_ones(kv[2 * s + 1][...]), None) for s in range(n_seg)]
    for sub in range(tq // unit):
        rows = slice(sub * unit, (sub + 1) * unit)
        for j in range(4):
            qb = q_ref[rows, j * LANES:(j + 1) * LANES]
            outs = []
            for half in range(2):
                q = jnp.where(head0, qb, zero) if half == 0 else jnp.where(head0, zero, qb)
                sink_col = (jnp.full((unit, 1), sink_ref[layer, j + 4 * half] * LOG2E, F32) if use_sink else None)
                outs.append(_attend(q, segs, sink_col))
            o_ref[rows, j * LANES:(j + 1) * LANES] = jnp.where(low, outs[0], outs[1]).astype(o_ref.dtype)


def _dense_split_kernel(*refs, tq, n_seg):
    q_ref = refs[0]
    kv = refs[1:1 + 2 * n_seg]
    o_ref = refs[-1]
    _, low = _pair_masks()
    unit = min(tq, ATTN_UNIT_ROWS)
    for j in range(4):
        vsl = slice(j * LANES, (j + 1) * LANES)
        vals = [_with_ones(kv[2 * s + 1][:, vsl]) for s in range(n_seg)]
        for sub in range(tq // unit):
            rows = slice(sub * unit, (sub + 1) * unit)
            outs = []
            for half in range(2):
                ksl = slice((2 * j + half) * LANES, (2 * j + half + 1) * LANES)
                segs = [(kv[2 * s][:, ksl], vals[s], None) for s in range(n_seg)]
                outs.append(_attend(q_ref[rows, ksl], segs, None))
            o_ref[rows, vsl] = jnp.where(low, outs[0], outs[1]).astype(o_ref.dtype)


def _window_kernel(q_ref, kl_ref, vl_ref, kc_ref, vc_ref, sink_ref, o_ref, *, layer, tq, n_sub):
    span = tq + 2 * WINDOW
    head0, low = _pair_masks()
    zero = jnp.zeros((), BF16)
    ctx_seg = (kc_ref[...], _with_ones(vc_ref[...]), None)
    for sub in range(n_sub):
        rows = slice(sub * tq, (sub + 1) * tq)
        start = (pl.program_id(1) * n_sub + sub) * tq
        lo = pl.multiple_of(jnp.clip(start - WINDOW, 0, SEQ - span), WINDOW)
        q_pos = start + lax.broadcasted_iota(jnp.int32, (tq, span), 0)
        k_pos = lo + lax.broadcasted_iota(jnp.int32, (tq, span), 1)
        band = jnp.where(jnp.abs(q_pos - k_pos) <= WINDOW, 0.0, -jnp.inf)
        band = jnp.concatenate([band, band], axis=0)
        segs = [(kl_ref[pl.ds(lo, span), :], _with_ones(vl_ref[pl.ds(lo, span), :]), band), ctx_seg]
        for j in range(4):
            qb = q_ref[rows, j * LANES:(j + 1) * LANES]
            q = jnp.concatenate([jnp.where(head0, qb, zero), jnp.where(head0, zero, qb)], axis=0)
            sink_col = jnp.concatenate([jnp.full((tq, 1), sink_ref[layer, j] * LOG2E, F32),
                                        jnp.full((tq, 1), sink_ref[layer, j + 4] * LOG2E, F32)], axis=0)
            o = _attend(q, segs, sink_col)
            o_ref[rows, j * LANES:(j + 1) * LANES] = jnp.where(low, o[:tq], o[tq:]).astype(o_ref.dtype)


def _attn_params(n_grid):
    return pltpu.CompilerParams(dimension_semantics=("parallel",) * n_grid, vmem_limit_bytes=VMEM_LIMIT)


def _latent_dense_shared(q, kvab, k_blk, v_blk, layer, name):
    tq = TQ_DENSE_SHARED
    nq = SEQ // tq
    kern = functools.partial(_dense_shared_kernel, layer=layer, tq=tq, n_seg=2, use_sink=False)
    return pl.pallas_call(
        kern,
        grid=(BATCH, nq),
        in_specs=[
            pl.BlockSpec((tq, 4 * LANES), lambda b, i: (b * nq + i, 0)),
            pl.BlockSpec((SEQ, LANES), lambda b, i: (b, k_blk)),
            pl.BlockSpec((SEQ, LANES), lambda b, i: (b, v_blk)),
            pl.BlockSpec((CTX_LEN, LANES), lambda b, i: (CTX_BLK0 + b, k_blk)),
            pl.BlockSpec((CTX_LEN, LANES), lambda b, i: (CTX_BLK0 + b, v_blk)),
        ],
        out_specs=pl.BlockSpec((tq, 4 * LANES), lambda b, i: (b * nq + i, 0)),
        out_shape=jax.ShapeDtypeStruct((N_LAT, 4 * LANES), BF16),
        compiler_params=_attn_params(2),
        name=name,
    )(q, kvab, kvab, kvab, kvab)


def _latent_window(q, kvab, sink, layer):
    n_sub = WINDOW_SUBTILES
    tq = TQ_WINDOW * n_sub
    nq = SEQ // tq
    kern = functools.partial(_window_kernel, layer=layer, tq=TQ_WINDOW, n_sub=n_sub)
    return pl.pallas_call(
        kern,
        grid=(BATCH, nq),
        in_specs=[
            pl.BlockSpec((tq, 4 * LANES), lambda b, i: (b * nq + i, 0)),
            pl.BlockSpec((SEQ, LANES), lambda b, i: (b, 0)),
            pl.BlockSpec((SEQ, LANES), lambda b, i: (b, 1)),
            pl.BlockSpec((CTX_LEN, LANES), lambda b, i: (CTX_BLK0 + b, 0)),
            pl.BlockSpec((CTX_LEN, LANES), lambda b, i: (CTX_BLK0 + b, 1)),
            _SMEM_SPEC,
        ],
        out_specs=pl.BlockSpec((tq, 4 * LANES), lambda b, i: (b * nq + i, 0)),
        out_shape=jax.ShapeDtypeStruct((N_LAT, 4 * LANES), BF16),
        compiler_params=_attn_params(2),
        name="attn_a_window",
    )(q, kvab, kvab, kvab, kvab, sink)


def _latent_dense_split(q, k, v):
    tq = TQ_DENSE_SPLIT
    nq = SEQ // tq
    kern = functools.partial(_dense_split_kernel, tq=tq, n_seg=2)
    return pl.pallas_call(
        kern,
        grid=(BATCH, nq),
        in_specs=[
            pl.BlockSpec((tq, 8 * LANES), lambda b, i: (b * nq + i, 0)),
            pl.BlockSpec((SEQ, 8 * LANES), lambda b, i: (b, 0)),
            pl.BlockSpec((SEQ, 4 * LANES), lambda b, i: (b, 0)),
            pl.BlockSpec((CTX_LEN, 8 * LANES), lambda b, i: (CTX_BLK0 + b, 0)),
            pl.BlockSpec((CTX_LEN, 4 * LANES), lambda b, i: (CTX_BLK0 + b, 0)),
        ],
        out_specs=pl.BlockSpec((tq, 4 * LANES), lambda b, i: (b * nq + i, 0)),
        out_shape=jax.ShapeDtypeStruct((N_LAT, 4 * LANES), BF16),
        compiler_params=_attn_params(2),
        name="attn_c_latent",
    )(q, k, v, k, v)


def _context_shared(q, kvab, k_blk, v_blk, sink, layer, name):
    tq = CTX_LEN
    use_sink = sink is not None
    kern = functools.partial(_dense_shared_kernel, layer=layer, tq=tq, n_seg=1, use_sink=use_sink)
    in_specs = [
        pl.BlockSpec((tq, 4 * LANES), lambda b: (CTX_BLK0 + b, 0)),
        pl.BlockSpec((CTX_LEN, LANES), lambda b: (CTX_BLK0 + b, k_blk)),
        pl.BlockSpec((CTX_LEN, LANES), lambda b: (CTX_BLK0 + b, v_blk)),
    ]
    args = [q, kvab, kvab]
    if use_sink:
        in_specs.append(_SMEM_SPEC)
        args.append(sink)
    return pl.pallas_call(
        kern,
        grid=(BATCH,),
        in_specs=in_specs,
        out_specs=pl.BlockSpec((tq, 4 * LANES), lambda b: (b, 0)),
        out_shape=jax.ShapeDtypeStruct((N_CTX, 4 * LANES), BF16),
        compiler_params=_attn_params(1),
        name=name,
    )(*args)


def _context_split(q, k, v):
    tq = CTX_LEN
    kern = functools.partial(_dense_split_kernel, tq=tq, n_seg=1)
    return pl.pallas_call(
        kern,
        grid=(BATCH,),
        in_specs=[
            pl.BlockSpec((tq, 8 * LANES), lambda b: (CTX_BLK0 + b, 0)),
            pl.BlockSpec((CTX_LEN, 8 * LANES), lambda b: (CTX_BLK0 + b, 0)),
            pl.BlockSpec((CTX_LEN, 4 * LANES), lambda b: (CTX_BLK0 + b, 0)),
        ],
        out_specs=pl.BlockSpec((tq, 4 * LANES), lambda b: (b, 0)),
        out_shape=jax.ShapeDtypeStruct((N_CTX, 4 * LANES), BF16),
        compiler_params=_attn_params(1),
        name="attn_c_context",
    )(q, k, v)


def _merge_kernel(x_ref, sh_ref, sc_ref, gt_ref, g_ref, wg_ref, *refs, split):
    n_y = 6 if split else 3
    y_refs, (wa_ref, wb_ref, wc_ref, wo_ref, o_ref) = refs[:n_y], refs[n_y:]
    x = x_ref[...]
    u = _adaln_bf16(x, g_ref[...], sh_ref[0], sc_ref[0])
    m = None
    for idx, w_ref in enumerate((wa_ref, wb_ref, wc_ref)):
        y = _pick_rows(y_refs[2 * idx], y_refs[2 * idx + 1], TM_MERGE) if split else y_refs[idx][...]
        gate = _dot(u, wg_ref[:, idx * D_MODEL:(idx + 1) * D_MODEL])
        term = jax.nn.sigmoid(gate) * _dot(y, w_ref[...])
        m = term if m is None else m + term
    o_ref[...] = x + gt_ref[0] * _dot(m.astype(BF16), wo_ref[...])


def _merge(x, mods, layer, gain, wg, ys, wa, wb, wc, wo, n_rows):
    tm = TM_MERGE
    split = isinstance(ys[0], tuple)

    def lmat(r, c):
        return pl.BlockSpec((None, r, c), lambda i: (layer, 0, 0))

    if split:
        y_specs = list(_split_row_specs(tm, 4 * LANES)) * 3
        y_args = [a for pair in ys for a in pair]
    else:
        y_specs = [pl.BlockSpec((tm, 4 * LANES), lambda i: (i, 0))] * 3
        y_args = list(ys)
    return pl.pallas_call(
        functools.partial(_merge_kernel, split=split),
        grid=(n_rows // tm,),
        in_specs=[
            pl.BlockSpec((tm, D_MODEL), lambda i: (i, 0)),
            _mod_spec(layer, 3, tm), _mod_spec(layer, 4, tm), _mod_spec(layer, 5, tm),
            pl.BlockSpec((None, 1, D_MODEL), lambda i: (layer, 0, 0)),
            lmat(D_MODEL, GATE_COLS),
            *y_specs,
            lmat(4 * LANES, D_MODEL), lmat(4 * LANES, D_MODEL), lmat(4 * LANES, D_MODEL),
            lmat(D_MODEL, D_MODEL),
        ],
        out_specs=pl.BlockSpec((tm, D_MODEL), lambda i: (i, 0)),
        out_shape=jax.ShapeDtypeStruct((n_rows, D_MODEL), F32),
        compiler_params=pltpu.CompilerParams(
            dimension_semantics=("parallel",), vmem_limit_bytes=VMEM_LIMIT),
        name="merge",
    )(x, mods, mods, mods, gain, wg, *y_args, wa, wb, wc, wo)


def kernel(x, c, ctx, c_ctx, ada_w, ada_b, ffn1_norm, ffn1_w_in, ffn1_w_out, mix_norm, mix_w_in, a_q_norm, a_k_norm, a_sink, b_q_norm, b_k_norm, c_q_lat_norm, c_w_uq, c_kv_lat_norm, c_w_ukv, c_q_norm, c_k_norm, w_branch_a, w_branch_b, w_branch_c, mix_w_out, ffn2_norm, ffn2_w_in, ffn2_w_out):
    n_l = mix_w_in.shape[0]
    wp = _permute_cols(_proj_weight_natural(mix_w_in), PROJ_BLOCK_KINDS)
    wg = mix_w_in[:, :, KV_COLS + Q_COLS:].astype(BF16)
    ukv = c_w_ukv.reshape(n_l, C_KV_RANK, N_HEADS, C_NOPE_DIM + C_V_DIM)
    wkc = _permute_cols(_pad_heads(ukv, C_NOPE_DIM), ("c",) * N_HEADS)
    wvc = ukv[..., C_NOPE_DIM:].reshape(n_l, C_KV_RANK, N_HEADS * C_V_DIM).astype(BF16)
    uq = c_w_uq.reshape(n_l, C_Q_RANK, N_HEADS, C_QK_DIM)
    wqc = _permute_cols(_pad_heads(uq, C_QK_DIM), ("c",) * N_HEADS)

    def pair_rows(w):
        return w.reshape(n_l, 2, 4, HEAD_DIM, D_MODEL).transpose(0, 2, 1, 3, 4).reshape(n_l, 8 * HEAD_DIM, D_MODEL)

    wa = pair_rows(w_branch_a).astype(BF16)
    wb = pair_rows(w_branch_b).astype(BF16)
    wc = w_branch_c.astype(BF16)
    wo = mix_w_out.astype(BF16)
    f1_in, f1_out = ffn1_w_in.astype(BF16), ffn1_w_out.astype(BF16)
    f2_in, f2_out = ffn2_w_in.astype(BF16), ffn2_w_out.astype(BF16)
    gains = (_pair_gain(a_k_norm), _pair_gain(a_q_norm), _pair_gain(b_k_norm), _pair_gain(b_q_norm),
             c_kv_lat_norm[:, None, :], c_q_lat_norm[:, None, :], _c_gain(c_k_norm), _c_gain(c_q_norm))
    tables = _rope_tables(TM_PROJ)
    ffn1_g, mix_g, ffn2_g = ffn1_norm[:, None, :], mix_norm[:, None, :], ffn2_norm[:, None, :]

    cond = jnp.concatenate([c, c_ctx[None, :], jnp.zeros((GROUP_PAD - N_GROUPS, D_MODEL), F32)], axis=0)
    mods = _modulation(cond, ada_w, ada_b).reshape(DEPTH * N_MOD * GROUP_PAD, 1, D_MODEL)

    xs = None
    for l in range(DEPTH):
        last = l == DEPTH - 1
        if l == 0:
            xs = _ffn(x.reshape(N_LAT, D_MODEL), mods, l, 0, ffn1_g, f1_in, f1_out, ctx.reshape(N_CTX, D_MODEL))
        else:
            xs = _ffn(xs, mods, l, 0, ffn1_g, f1_in, f1_out)
        kvab, qa, qb, qc, kc, vc = _project(xs, mods, l, mix_g, wp, wkc, wvc, wqc, gains, tables)
        ys = [_latent_window(qa, kvab, a_sink, l),
              _latent_dense_shared(qb, kvab, 2, 3, l, "attn_b_latent"),
              _latent_dense_split(qc, kc, vc)]
        if not last:
            ys = list(zip(ys, [_context_shared(qa, kvab, 0, 1, a_sink, l, "attn_a_context"),
                               _context_shared(qb, kvab, 2, 3, None, l, "attn_b_context"),
                               _context_split(qc, kc, vc)]))
        xs = _merge(xs, mods, l, mix_g, wg, ys, wa, wb, wc, wo, N_LAT if last else N_ROWS)
        xs = _ffn(xs, mods, l, 6, ffn2_g, f2_in, f2_out)
    return xs.reshape(BATCH, SEQ, D_MODEL)
```

```python
import functools
import math

import numpy as np
import jax
import jax.numpy as jnp
from jax import lax
from jax.experimental import pallas as pl
from jax.experimental.pallas import tpu as pltpu

F32 = jnp.float32
BF16 = jnp.bfloat16

D_MODEL = 1024
BATCH = 8
SEQ = 2048
DEPTH = 2
GRID_W = 64
CTX_LEN = 256
HEAD_DIM = 64
WINDOW = 128
ROPE_THETA = 10000.0
NORM_EPS = 1e-6
N_MOD = 9
D_FF = 2816
N_HEADS = 8
C_NOPE_DIM = 64
C_ROPE_DIM = 32
C_QK_DIM = C_NOPE_DIM + C_ROPE_DIM
C_V_DIM = 64
C_Q_RANK = 384
C_KV_RANK = 256
KV_COLS = 800
Q_COLS = 1408
GATE_COLS = 3 * D_MODEL

LANES = 128
N_LAT = BATCH * SEQ
N_CTX = BATCH * CTX_LEN
N_ROWS = N_LAT + N_CTX
N_GROUPS = BATCH + 1
GROUP_PAD = 16
CTX_BLK0 = N_LAT // CTX_LEN

VMEM_LIMIT = 56 * 1024 * 1024
LOG2E = math.log2(math.e)

PB_AK, PB_AV, PB_BK, PB_BV, PB_CKV, PB_AQ, PB_BQ, PB_CKR, PB_CQ, PB_END = 0, 1, 2, 3, 4, 6, 10, 14, 15, 18
PROJ_COLS = PB_END * LANES

TM_FFN = 512
TM_PROJ = 512
TM_MERGE = 1024
TQ_DENSE_SHARED = 1024
TQ_DENSE_SPLIT = 1024
ATTN_UNIT_ROWS = 512
TQ_WINDOW = 256
WINDOW_SUBTILES = 4

_SMEM_SPEC = pl.BlockSpec(memory_space=pltpu.SMEM)


def _proj_weight_natural(w_in):
    def zeros(n):
        return jnp.zeros(w_in.shape[:-1] + (n,), w_in.dtype)

    q0 = KV_COLS
    blocks = [w_in[..., 0:768]]
    for base in (q0, q0 + 512):
        for j in range(4):
            blocks += [w_in[..., base + j * 64:base + (j + 1) * 64],
                       w_in[..., base + (j + 4) * 64:base + (j + 5) * 64]]
    blocks += [zeros(C_NOPE_DIM), w_in[..., 768:800], zeros(32)]
    blocks.append(w_in[..., q0 + 1024:q0 + 1024 + C_Q_RANK])
    return jnp.concatenate(blocks, axis=-1).astype(BF16)


PROJ_BLOCK_KINDS = (("pair", "id") * 2 + ("id",) * 2 + ("pair",) * 8 + ("c",) + ("id",) * 3)


def _pad_heads(w, n_used):
    lead = w.shape[:-1]
    blk = jnp.concatenate([w[..., :n_used], jnp.zeros(lead + (LANES - n_used,), w.dtype)], axis=-1)
    return blk.reshape(lead[:-1] + (N_HEADS * LANES,)).astype(BF16)


def _perm_matrix(kind):
    r = lax.broadcasted_iota(jnp.int32, (LANES, LANES), 0)
    c = lax.broadcasted_iota(jnp.int32, (LANES, LANES), 1)
    if kind == "pair":
        src = ((c >> 5) & 1) * 64 + 2 * (c & 31) + (c >> 6)
    else:
        src = jnp.where(c < 16, 64 + 2 * c,
                        jnp.where(c < 64, c - 16,
                                  jnp.where(c < 80, 2 * c - 63,
                                            jnp.where(c < 96, c - 32, -1))))
    return jnp.where(r == src, 1.0, 0.0).astype(BF16)


def _permute_kernel(w_ref, o_ref, *, kinds):
    mats = {k: _perm_matrix(k) for k in set(kinds) if k != "id"}
    for j, kind in enumerate(kinds):
        sl = slice(j * LANES, (j + 1) * LANES)
        if kind == "id":
            o_ref[:, sl] = w_ref[:, sl]
        else:
            o_ref[:, sl] = _dot(w_ref[:, sl], mats[kind]).astype(BF16)


def _permute_cols(w, kinds):
    n_l, r, c = w.shape
    spec = pl.BlockSpec((None, r, c), lambda l: (l, 0, 0))
    return pl.pallas_call(
        functools.partial(_permute_kernel, kinds=kinds),
        grid=(n_l,),
        in_specs=[spec],
        out_specs=spec,
        out_shape=jax.ShapeDtypeStruct(w.shape, BF16),
        compiler_params=pltpu.CompilerParams(dimension_semantics=("parallel",), vmem_limit_bytes=VMEM_LIMIT),
        name="permute_cols",
    )(w)


def _pair_gain(g):
    e, o = g[:, 0::2], g[:, 1::2]
    return jnp.stack([jnp.concatenate([e, e, o, o], axis=-1), jnp.concatenate([o, o, e, e], axis=-1)], axis=1)


def _c_gain(g):
    nope, rope = g[:, :C_NOPE_DIM], g[:, C_NOPE_DIM:]
    zeros = jnp.zeros((g.shape[0], 32), g.dtype)
    lo = jnp.concatenate([rope[:, 0::2], nope[:, :48]], axis=-1)
    hi = jnp.concatenate([rope[:, 1::2], nope[:, 48:], zeros], axis=-1)
    return jnp.stack([jnp.concatenate([lo, hi], axis=-1), jnp.concatenate([hi, lo], axis=-1)], axis=1)


def _rope_tables(tm):
    pos = np.arange(SEQ)
    row = (pos // GRID_W).astype(np.float32)
    col = (pos % GRID_W).astype(np.float32)

    def angles(rot_dim):
        n_freq = rot_dim // 4
        inv = np.float64(ROPE_THETA) ** (-np.arange(n_freq, dtype=np.float64) / n_freq)
        return np.concatenate([row[:, None] * inv, col[:, None] * inv], axis=-1)

    a64 = angles(HEAD_DIM)
    c, s = np.cos(a64), np.sin(a64)
    cos_ab = np.concatenate([c, c, c, c], axis=-1)
    sin_ab = np.concatenate([-s, -s, s, s], axis=-1)
    a32 = angles(C_ROPE_DIM)
    c, s = np.cos(a32), np.sin(a32)
    one48, zero48 = np.ones((SEQ, 48)), np.zeros((SEQ, 48))
    cos_c = np.concatenate([c, one48, c, one48], axis=-1)
    sin_c = np.concatenate([-s, zero48, s, zero48], axis=-1)
    ident_c, ident_s = np.ones((tm, LANES)), np.zeros((tm, LANES))
    return tuple(jnp.asarray(np.concatenate(pair).astype(np.float32))
                 for pair in ((cos_ab, ident_c), (sin_ab, ident_s), (cos_c, ident_c), (sin_c, ident_s)))


def _adaln_bf16(x, gain, shift, scale):
    inv = lax.rsqrt(jnp.mean(x * x, axis=-1, keepdims=True) + NORM_EPS)
    return ((x * inv) * (gain * (1.0 + scale)) + shift).astype(BF16)


def _rms(x, gain, n):
    return x * lax.rsqrt(jnp.sum(x * x, axis=-1, keepdims=True) * (1.0 / n) + NORM_EPS) * gain


def _rope_gain_tables(g_ref, cos, sin, scale):
    return cos * (g_ref[0:1, :] * scale), sin * (g_ref[1:2, :] * scale)


def _rotate_bf16(y, gcos, gsin):
    return (y * gcos + pltpu.roll(y, 64, axis=1) * gsin).astype(BF16)


def _group_ones(width, group_of):
    r = lax.broadcasted_iota(jnp.int32, (width, width), 0)
    c = lax.broadcasted_iota(jnp.int32, (width, width), 1)
    return jnp.where(group_of(r) == group_of(c), 1.0, 0.0).astype(BF16)


def _inv_norm_mxu(x, ones_mat, n):
    return lax.rsqrt(_dot((x * x).astype(BF16), ones_mat) + n * NORM_EPS)


def _dot(a, b):
    return jnp.dot(a, b, preferred_element_type=F32)


def _dot_nt(a, b):
    return lax.dot_general(a, b, (((1,), (1,)), ((), ())), preferred_element_type=F32)


def _mod_kernel(cond_ref, w_ref, b_ref, o_ref):
    c = cond_ref[...]
    s = (c * jax.nn.sigmoid(c)).astype(BF16)
    o_ref[...] = _dot(s, w_ref[...].astype(BF16)) + b_ref[...]


def _modulation(cond, ada_w, ada_b):
    return pl.pallas_call(
        _mod_kernel,
        grid=(DEPTH, N_MOD),
        in_specs=[
            pl.BlockSpec((GROUP_PAD, D_MODEL), lambda l, k: (0, 0)),
            pl.BlockSpec((None, D_MODEL, D_MODEL), lambda l, k: (l, 0, k)),
            pl.BlockSpec((None, 1, D_MODEL), lambda l, k: (l, 0, k)),
        ],
        out_specs=pl.BlockSpec((None, None, GROUP_PAD, D_MODEL), lambda l, k: (l, k, 0, 0)),
        out_shape=jax.ShapeDtypeStruct((DEPTH, N_MOD, GROUP_PAD, D_MODEL), F32),
        compiler_params=pltpu.CompilerParams(
            dimension_semantics=("arbitrary", "arbitrary"), vmem_limit_bytes=VMEM_LIMIT),
        name="modulation",
    )(cond, ada_w, ada_b.reshape(DEPTH, 1, N_MOD * D_MODEL))


def _mod_spec(layer, k, tm):
    base = (layer * N_MOD + k) * GROUP_PAD
    return pl.BlockSpec((1, 1, D_MODEL), lambda i: (base + (i * tm) // SEQ, 0, 0))


def _split_row_specs(tm, cols):
    n_lat = N_LAT // tm
    return (pl.BlockSpec((tm, cols), lambda i: (jnp.minimum(i, n_lat - 1), 0)),
            pl.BlockSpec((tm, cols), lambda i: (jnp.maximum(i - n_lat, 0), 0)))


def _pick_rows(lat_ref, ctx_ref, tm):
    return jnp.where(pl.program_id(0) >= N_LAT // tm, ctx_ref[...], lat_ref[...])


def _ffn_kernel(*refs, split):
    if split:
        x = _pick_rows(refs[0], refs[1], TM_FFN)
        refs = refs[1:]
    else:
        x = refs[0][...]
    _, sh_ref, sc_ref, gt_ref, g_ref, wi_ref, wo_ref, o_ref = refs
    h = _adaln_bf16(x, g_ref[...], sh_ref[0], sc_ref[0])
    gate = _dot(h, wi_ref[:, :D_FF])
    up = _dot(h, wi_ref[:, D_FF:])
    act = (gate * jax.nn.sigmoid(gate) * up).astype(BF16)
    o_ref[...] = x + 0.5 * gt_ref[0] * _dot(act, wo_ref[...])


def _ffn(x, mods, layer, k0, gain, w_in, w_out, x_ctx=None):
    tm = TM_FFN
    split = x_ctx is not None
    n_rows = x.shape[0] + (x_ctx.shape[0] if split else 0)
    resident = pl.Buffered(1)
    x_specs = list(_split_row_specs(tm, D_MODEL)) if split else [pl.BlockSpec((tm, D_MODEL), lambda i: (i, 0))]
    return pl.pallas_call(
        functools.partial(_ffn_kernel, split=split),
        grid=(n_rows // tm,),
        in_specs=x_specs + [
            _mod_spec(layer, k0, tm), _mod_spec(layer, k0 + 1, tm), _mod_spec(layer, k0 + 2, tm),
            pl.BlockSpec((None, 1, D_MODEL), lambda i: (layer, 0, 0)),
            pl.BlockSpec((None, D_MODEL, 2 * D_FF), lambda i: (layer, 0, 0), pipeline_mode=resident),
            pl.BlockSpec((None, D_FF, D_MODEL), lambda i: (layer, 0, 0), pipeline_mode=resident),
        ],
        out_specs=pl.BlockSpec((tm, D_MODEL), lambda i: (i, 0)),
        out_shape=jax.ShapeDtypeStruct((n_rows, D_MODEL), F32),
        compiler_params=pltpu.CompilerParams(
            dimension_semantics=("parallel",), vmem_limit_bytes=VMEM_LIMIT),
        name="ffn",
    )(*([x, x_ctx] if split else [x]), mods, mods, mods, gain, w_in, w_out)


ST_KVAB, ST_QA, ST_QB, ST_KC, ST_QC, ST_VC, ST_END = 0, 4, 8, 12, 20, 28, 32


def _proj_kernel(x_ref, sh_ref, sc_ref, g_ref, wp_ref, wkc_ref, wvc_ref, wqc_ref,
                 gak_ref, gaq_ref, gbk_ref, gbq_ref, gkvl_ref, gql_ref, gck_ref, gcq_ref,
                 cab_ref, sab_ref, cc_ref, sc_c_ref,
                 kvab_ref, qa_ref, qb_ref, qc_ref, kc_ref, vc_ref, stage_ref):
    @pl.when(pl.program_id(0) == 0)
    def _():
        stage_ref[...] = jnp.zeros_like(stage_ref)

    def staged(b, n=1):
        return stage_ref[:, b * LANES:(b + n) * LANES]

    def stage(b, val):
        stage_ref[:, b * LANES:b * LANES + val.shape[1]] = val

    cab, sab, cc, sc_c = cab_ref[...], sab_ref[...], cc_ref[...], sc_c_ref[...]
    root_hd, root_c = HEAD_DIM ** 0.5, C_QK_DIM ** 0.5
    for b, g_ref_k in ((0, gak_ref), (2, gbk_ref)):
        gcos, gsin = _rope_gain_tables(g_ref_k, cab, sab, root_hd)
        kvab_ref[:, b * LANES:(b + 1) * LANES] = _rotate_bf16(staged(ST_KVAB + b), gcos, gsin)
        kvab_ref[:, (b + 1) * LANES:(b + 2) * LANES] = staged(ST_KVAB + b + 1).astype(BF16)
    for st, g_ref_q, q_ref in ((ST_QA, gaq_ref, qa_ref), (ST_QB, gbq_ref, qb_ref)):
        gcos, gsin = _rope_gain_tables(g_ref_q, cab, sab, root_hd * HEAD_DIM ** -0.5 * LOG2E)
        for j in range(4):
            q_ref[:, j * LANES:(j + 1) * LANES] = _rotate_bf16(staged(st + j), gcos, gsin)
    gcos_k, gsin_k = _rope_gain_tables(gck_ref, cc, sc_c, root_c)
    gcos_q, gsin_q = _rope_gain_tables(gcq_ref, cc, sc_c, root_c * C_QK_DIM ** -0.5 * LOG2E)
    for hd in range(N_HEADS):
        sl = slice(hd * LANES, (hd + 1) * LANES)
        kc_ref[:, sl] = _rotate_bf16(staged(ST_KC + hd), gcos_k, gsin_k)
        qc_ref[:, sl] = _rotate_bf16(staged(ST_QC + hd), gcos_q, gsin_q)
    vc_ref[...] = staged(ST_VC, 4).astype(BF16)

    h = _adaln_bf16(x_ref[...], g_ref[...], sh_ref[0], sc_ref[0])

    raw = _dot(h, wp_ref[...])

    def project(b, n):
        return raw[:, b * LANES:(b + n) * LANES]

    pair_ones = _group_ones(LANES, lambda l: (l >> 5) & 1)
    pair_ones2 = _group_ones(2 * LANES, lambda l: ((l >> 5) & 1) + 2 * (l >> 7))

    for b in (PB_AK, PB_BK):
        r = project(b, 2)
        k = r[:, :LANES]
        stage(ST_KVAB + b, k * _inv_norm_mxu(k, pair_ones, HEAD_DIM))
        stage(ST_KVAB + b + 1, r[:, LANES:])
    for st, pb in ((ST_QA, PB_AQ), (ST_QB, PB_BQ)):
        for j in (0, 2):
            r = project(pb + j, 2)
            stage(st + j, r * _inv_norm_mxu(r, pair_ones2, HEAD_DIM))

    ckv = _rms(project(PB_CKV, 2), gkvl_ref[...], C_KV_RANK).astype(BF16)
    stage(ST_VC, _dot(ckv, wvc_ref[...]))
    r = project(PB_CKR, 4)
    cq = _rms(r[:, LANES:], gql_ref[...], C_Q_RANK).astype(BF16)
    k_up = _dot(ckv, wkc_ref[...])
    q_up = _dot(cq, wqc_ref[...])
    k_rope = r[:, :LANES]

    def unit_norm(v):
        return v * lax.rsqrt(jnp.sum(v * v, axis=-1, keepdims=True) + C_QK_DIM * NORM_EPS)

    for hd in range(N_HEADS):
        sl = slice(hd * LANES, (hd + 1) * LANES)
        stage(ST_KC + hd, unit_norm(k_up[:, sl] + k_rope))
        stage(ST_QC + hd, unit_norm(q_up[:, sl]))


def _project(x, mods, layer, gain, wp, wkc, wvc, wqc, gains, tables):
    tm = TM_PROJ
    n_tiles = N_ROWS // tm
    n_lat_tiles = N_LAT // tm
    tiles_per_seq = SEQ // tm

    def cur(i):
        return jnp.minimum(i, n_tiles - 1)

    def prev(i):
        return jnp.maximum(i - 1, 0)

    def tab_idx(i):
        t = prev(i)
        return (jnp.where(t < n_lat_tiles, t % tiles_per_seq, tiles_per_seq), 0)

    def mod_spec(k):
        base = (layer * N_MOD + k) * GROUP_PAD
        return pl.BlockSpec((1, 1, D_MODEL), lambda i: (base + (cur(i) * tm) // SEQ, 0, 0))

    def lvec(n):
        return pl.BlockSpec((None, 1, n), lambda i: (layer, 0, 0))

    def lmat(r, c, **kw):
        return pl.BlockSpec((None, r, c), lambda i: (layer, 0, 0), **kw)

    def rows(c):
        return pl.BlockSpec((tm, c), lambda i: (prev(i), 0)), jax.ShapeDtypeStruct((N_ROWS, c), BF16)

    out = [rows(4 * LANES), rows(4 * LANES), rows(4 * LANES), rows(8 * LANES), rows(8 * LANES), rows(4 * LANES)]
    tab_spec = pl.BlockSpec((tm, LANES), tab_idx)
    resident = dict(pipeline_mode=pl.Buffered(1))
    return pl.pallas_call(
        _proj_kernel,
        grid=(n_tiles + 1,),
        in_specs=[
            pl.BlockSpec((tm, D_MODEL), lambda i: (cur(i), 0)),
            mod_spec(3), mod_spec(4),
            lvec(D_MODEL),
            lmat(D_MODEL, PROJ_COLS, **resident), lmat(C_KV_RANK, 8 * LANES, **resident),
            lmat(C_KV_RANK, 4 * LANES, **resident), lmat(C_Q_RANK, 8 * LANES, **resident),
            lmat(2, LANES), lmat(2, LANES), lmat(2, LANES), lmat(2, LANES), lvec(C_KV_RANK), lvec(C_Q_RANK),
            lmat(2, LANES), lmat(2, LANES),
            tab_spec, tab_spec, tab_spec, tab_spec,
        ],
        out_specs=[o[0] for o in out],
        out_shape=[o[1] for o in out],
        scratch_shapes=[pltpu.VMEM((tm, ST_END * LANES), F32)],
        compiler_params=pltpu.CompilerParams(
            dimension_semantics=("arbitrary",), vmem_limit_bytes=VMEM_LIMIT),
        name="project",
    )(x, mods, mods, gain, wp, wkc, wvc, wqc, *gains, *tables)


def _with_ones(v):
    return jnp.concatenate([v, jnp.ones_like(v)], axis=1)


def _attend(q, segs, sink_col):
    scores = []
    for k, _, bias in segs:
        s = _dot_nt(q, k)
        scores.append(s if bias is None else s + bias)
    m = functools.reduce(jnp.maximum, [jnp.max(s, axis=-1, keepdims=True) for s in scores])
    if sink_col is not None:
        m = jnp.maximum(m, sink_col)
    acc = None
    for s, (_, v, _) in zip(scores, segs):
        part = _dot(jnp.exp2(s - m).astype(BF16), v)
        acc = part if acc is None else acc + part
    denom = acc[:, LANES:]
    if sink_col is not None:
        denom = denom + jnp.exp2(sink_col - m)
    return acc[:, :LANES] / denom


def _pair_masks():
    lane = lax.broadcasted_iota(jnp.int32, (1, LANES), 1)
    return (lane & 63) < 32, lane < 64


def _dense_shared_kernel(*refs, layer, tq, n_seg, use_sink):
    q_ref = refs[0]
    kv = refs[1:1 + 2 * n_seg]
    sink_ref = refs[1 + 2 * n_seg] if use_sink else None
    o_ref = refs[-1]
    head0, low = _pair_masks()
    zero = jnp.zeros((), BF16)
    unit = min(tq, ATTN_UNIT_ROWS)
    segs = [(kv[2 * s][...], _with_ones(kv[2 * s + 1][...]), None) for s in range(n_seg)]
    for sub in range(tq // unit):
        rows = slice(sub * unit, (sub + 1) * unit)
        for j in range(4):
            qb = q_ref[rows, j * LANES:(j + 1) * LANES]
            outs = []
            for half in range(2):
                q = jnp.where(head0, qb, zero) if half == 0 else jnp.where(head0, zero, qb)
                sink_col = (jnp.full((unit, 1), sink_ref[layer, j + 4 * half] * LOG2E, F32) if use_sink else None)
                outs.append(_attend(q, segs, sink_col))
            o_ref[rows, j * LANES:(j + 1) * LANES] = jnp.where(low, outs[0], outs[1]).astype(o_ref.dtype)


def _dense_split_kernel(*refs, tq, n_seg):
    q_ref = refs[0]
    kv = refs[1:1 + 2 * n_seg]
    o_ref = refs[-1]
    _, low = _pair_masks()
    unit = min(tq, ATTN_UNIT_ROWS)
    for j in range(4):
        vsl = slice(j * LANES, (j + 1) * LANES)
        vals = [_with_ones(kv[2 * s + 1][:, vsl]) for s in range(n_seg)]
        for sub in range(tq // unit):
            rows = slice(sub * unit, (sub + 1) * unit)
            outs = []
            for half in range(2):
                ksl = slice((2 * j + half) * LANES, (2 * j + half + 1) * LANES)
                segs = [(kv[2 * s][:, ksl], vals[s], None) for s in range(n_seg)]
                outs.append(_attend(q_ref[rows, ksl], segs, None))
            o_ref[rows, vsl] = jnp.where(low, outs[0], outs[1]).astype(o_ref.dtype)


def _window_kernel(q_ref, kl_ref, vl_ref, kc_ref, vc_ref, sink_ref, o_ref, *, layer, tq, n_sub):
    span = tq + 2 * WINDOW
    head0, low = _pair_masks()
    zero = jnp.zeros((), BF16)
    ctx_seg = (kc_ref[...], _with_ones(vc_ref[...]), None)
    for sub in range(n_sub):
        rows = slice(sub * tq, (sub + 1) * tq)
        start = (pl.program_id(1) * n_sub + sub) * tq
        lo = pl.multiple_of(jnp.clip(start - WINDOW, 0, SEQ - span), WINDOW)
        q_pos = start + lax.broadcasted_iota(jnp.int32, (tq, span), 0)
        k_pos = lo + lax.broadcasted_iota(jnp.int32, (tq, span), 1)
        band = jnp.where(jnp.abs(q_pos - k_pos) <= WINDOW, 0.0, -jnp.inf)
        band = jnp.concatenate([band, band], axis=0)
        segs = [(kl_ref[pl.ds(lo, span), :], _with_ones(vl_ref[pl.ds(lo, span), :]), band), ctx_seg]
        for j in range(4):
            qb = q_ref[rows, j * LANES:(j + 1) * LANES]
            q = jnp.concatenate([jnp.where(head0, qb, zero), jnp.where(head0, zero, qb)], axis=0)
            sink_col = jnp.concatenate([jnp.full((tq, 1), sink_ref[layer, j] * LOG2E, F32),
                                        jnp.full((tq, 1), sink_ref[layer, j + 4] * LOG2E, F32)], axis=0)
            o = _attend(q, segs, sink_col)
            o_ref[rows, j * LANES:(j + 1) * LANES] = jnp.where(low, o[:tq], o[tq:]).astype(o_ref.dtype)


def _attn_params(n_grid):
    return pltpu.CompilerParams(dimension_semantics=("parallel",) * n_grid, vmem_limit_bytes=VMEM_LIMIT)


def _latent_dense_shared(q, kvab, k_blk, v_blk, layer, name):
    tq = TQ_DENSE_SHARED
    nq = SEQ // tq
    kern = functools.partial(_dense_shared_kernel, layer=layer, tq=tq, n_seg=2, use_sink=False)
    return pl.pallas_call(
        kern,
        grid=(BATCH, nq),
        in_specs=[
            pl.BlockSpec((tq, 4 * LANES), lambda b, i: (b * nq + i, 0)),
            pl.BlockSpec((SEQ, LANES), lambda b, i: (b, k_blk)),
            pl.BlockSpec((SEQ, LANES), lambda b, i: (b, v_blk)),
            pl.BlockSpec((CTX_LEN, LANES), lambda b, i: (CTX_BLK0 + b, k_blk)),
            pl.BlockSpec((CTX_LEN, LANES), lambda b, i: (CTX_BLK0 + b, v_blk)),
        ],
        out_specs=pl.BlockSpec((tq, 4 * LANES), lambda b, i: (b * nq + i, 0)),
        out_shape=jax.ShapeDtypeStruct((N_LAT, 4 * LANES), BF16),
        compiler_params=_attn_params(2),
        name=name,
    )(q, kvab, kvab, kvab, kvab)


def _latent_window(q, kvab, sink, layer):
    n_sub = WINDOW_SUBTILES
    tq = TQ_WINDOW * n_sub
    nq = SEQ // tq
    kern = functools.partial(_window_kernel, layer=layer, tq=TQ_WINDOW, n_sub=n_sub)
    return pl.pallas_call(
        kern,
        grid=(BATCH, nq),
        in_specs=[
            pl.BlockSpec((tq, 4 * LANES), lambda b, i: (b * nq + i, 0)),
            pl.BlockSpec((SEQ, LANES), lambda b, i: (b, 0)),
            pl.BlockSpec((SEQ, LANES), lambda b, i: (b, 1)),
            pl.BlockSpec((CTX_LEN, LANES), lambda b, i: (CTX_BLK0 + b, 0)),
            pl.BlockSpec((CTX_LEN, LANES), lambda b, i: (CTX_BLK0 + b, 1)),
            _SMEM_SPEC,
        ],
        out_specs=pl.BlockSpec((tq, 4 * LANES), lambda b, i: (b * nq + i, 0)),
        out_shape=jax.ShapeDtypeStruct((N_LAT, 4 * LANES), BF16),
        compiler_params=_attn_params(2),
        name="attn_a_window",
    )(q, kvab, kvab, kvab, kvab, sink)


def _latent_dense_split(q, k, v):
    tq = TQ_DENSE_SPLIT
    nq = SEQ // tq
    kern = functools.partial(_dense_split_kernel, tq=tq, n_seg=2)
    return pl.pallas_call(
        kern,
        grid=(BATCH, nq),
        in_specs=[
            pl.BlockSpec((tq, 8 * LANES), lambda b, i: (b * nq + i, 0)),
            pl.BlockSpec((SEQ, 8 * LANES), lambda b, i: (b, 0)),
            pl.BlockSpec((SEQ, 4 * LANES), lambda b, i: (b, 0)),
            pl.BlockSpec((CTX_LEN, 8 * LANES), lambda b, i: (CTX_BLK0 + b, 0)),
            pl.BlockSpec((CTX_LEN, 4 * LANES), lambda b, i: (CTX_BLK0 + b, 0)),
        ],
        out_specs=pl.BlockSpec((tq, 4 * LANES), lambda b, i: (b * nq + i, 0)),
        out_shape=jax.ShapeDtypeStruct((N_LAT, 4 * LANES), BF16),
        compiler_params=_attn_params(2),
        name="attn_c_latent",
    )(q, k, v, k, v)


def _context_shared(q, kvab, k_blk, v_blk, sink, layer, name):
    tq = CTX_LEN
    use_sink = sink is not None
    kern = functools.partial(_dense_shared_kernel, layer=layer, tq=tq, n_seg=1, use_sink=use_sink)
    in_specs = [
        pl.BlockSpec((tq, 4 * LANES), lambda b: (CTX_BLK0 + b, 0)),
        pl.BlockSpec((CTX_LEN, LANES), lambda b: (CTX_BLK0 + b, k_blk)),
        pl.BlockSpec((CTX_LEN, LANES), lambda b: (CTX_BLK0 + b, v_blk)),
    ]
    args = [q, kvab, kvab]
    if use_sink:
        in_specs.append(_SMEM_SPEC)
        args.append(sink)
    return pl.pallas_call(
        kern,
        grid=(BATCH,),
        in_specs=in_specs,
        out_specs=pl.BlockSpec((tq, 4 * LANES), lambda b: (b, 0)),
        out_shape=jax.ShapeDtypeStruct((N_CTX, 4 * LANES), BF16),
        compiler_params=_attn_params(1),
        name=name,
    )(*args)


def _context_split(q, k, v):
    tq = CTX_LEN
    kern = functools.partial(_dense_split_kernel, tq=tq, n_seg=1)
    return pl.pallas_call(
        kern,
        grid=(BATCH,),
        in_specs=[
            pl.BlockSpec((tq, 8 * LANES), lambda b: (CTX_BLK0 + b, 0)),
            pl.BlockSpec((CTX_LEN, 8 * LANES), lambda b: (CTX_BLK0 + b, 0)),
            pl.BlockSpec((CTX_LEN, 4 * LANES), lambda b: (CTX_BLK0 + b, 0)),
        ],
        out_specs=pl.BlockSpec((tq, 4 * LANES), lambda b: (b, 0)),
        out_shape=jax.ShapeDtypeStruct((N_CTX, 4 * LANES), BF16),
        compiler_params=_attn_params(1),
        name="attn_c_context",
    )(q, k, v)


def _merge_kernel(x_ref, sh_ref, sc_ref, gt_ref, g_ref, wg_ref, *refs, split):
    n_y = 6 if split else 3
    y_refs, (wa_ref, wb_ref, wc_ref, wo_ref, o_ref) = refs[:n_y], refs[n_y:]
    x = x_ref[...]
    u = _adaln_bf16(x, g_ref[...], sh_ref[0], sc_ref[0])
    m = None
    for idx, w_ref in enumerate((wa_ref, wb_ref, wc_ref)):
        y = _pick_rows(y_refs[2 * idx], y_refs[2 * idx + 1], TM_MERGE) if split else y_refs[idx][...]
        gate = _dot(u, wg_ref[:, idx * D_MODEL:(idx + 1) * D_MODEL])
        term = jax.nn.sigmoid(gate) * _dot(y, w_ref[...])
        m = term if m is None else m + term
    o_ref[...] = x + gt_ref[0] * _dot(m.astype(BF16), wo_ref[...])


def _merge(x, mods, layer, gain, wg, ys, wa, wb, wc, wo, n_rows):
    tm = TM_MERGE
    split = isinstance(ys[0], tuple)

    def lmat(r, c):
        return pl.BlockSpec((None, r, c), lambda i: (layer, 0, 0))

    if split:
        y_specs = list(_split_row_specs(tm, 4 * LANES)) * 3
        y_args = [a for pair in ys for a in pair]
    else:
        y_specs = [pl.BlockSpec((tm, 4 * LANES), lambda i: (i, 0))] * 3
        y_args = list(ys)
    return pl.pallas_call(
        functools.partial(_merge_kernel, split=split),
        grid=(n_rows // tm,),
        in_specs=[
            pl.BlockSpec((tm, D_MODEL), lambda i: (i, 0)),
            _mod_spec(layer, 3, tm), _mod_spec(layer, 4, tm), _mod_spec(layer, 5, tm),
            pl.BlockSpec((None, 1, D_MODEL), lambda i: (layer, 0, 0)),
            lmat(D_MODEL, GATE_COLS),
            *y_specs,
            lmat(4 * LANES, D_MODEL), lmat(4 * LANES, D_MODEL), lmat(4 * LANES, D_MODEL),
            lmat(D_MODEL, D_MODEL),
        ],
        out_specs=pl.BlockSpec((tm, D_MODEL), lambda i: (i, 0)),
        out_shape=jax.ShapeDtypeStruct((n_rows, D_MODEL), F32),
        compiler_params=pltpu.CompilerParams(
            dimension_semantics=("parallel",), vmem_limit_bytes=VMEM_LIMIT),
        name="merge",
    )(x, mods, mods, mods, gain, wg, *y_args, wa, wb, wc, wo)


def kernel(x, c, ctx, c_ctx, ada_w, ada_b, ffn1_norm, ffn1_w_in, ffn1_w_out, mix_norm, mix_w_in, a_q_norm, a_k_norm, a_sink, b_q_norm, b_k_norm, c_q_lat_norm, c_w_uq, c_kv_lat_norm, c_w_ukv, c_q_norm, c_k_norm, w_branch_a, w_branch_b, w_branch_c, mix_w_out, ffn2_norm, ffn2_w_in, ffn2_w_out):
    n_l = mix_w_in.shape[0]
    wp = _permute_cols(_proj_weight_natural(mix_w_in), PROJ_BLOCK_KINDS)
    wg = mix_w_in[:, :, KV_COLS + Q_COLS:].astype(BF16)
    ukv = c_w_ukv.reshape(n_l, C_KV_RANK, N_HEADS, C_NOPE_DIM + C_V_DIM)
    wkc = _permute_cols(_pad_heads(ukv, C_NOPE_DIM), ("c",) * N_HEADS)
    wvc = ukv[..., C_NOPE_DIM:].reshape(n_l, C_KV_RANK, N_HEADS * C_V_DIM).astype(BF16)
    uq = c_w_uq.reshape(n_l, C_Q_RANK, N_HEADS, C_QK_DIM)
    wqc = _permute_cols(_pad_heads(uq, C_QK_DIM), ("c",) * N_HEADS)

    def pair_rows(w):
        return w.reshape(n_l, 2, 4, HEAD_DIM, D_MODEL).transpose(0, 2, 1, 3, 4).reshape(n_l, 8 * HEAD_DIM, D_MODEL)

    wa = pair_rows(w_branch_a).astype(BF16)
    wb = pair_rows(w_branch_b).astype(BF16)
    wc = w_branch_c.astype(BF16)
    wo = mix_w_out.astype(BF16)
    f1_in, f1_out = ffn1_w_in.astype(BF16), ffn1_w_out.astype(BF16)
    f2_in, f2_out = ffn2_w_in.astype(BF16), ffn2_w_out.astype(BF16)
    gains = (_pair_gain(a_k_norm), _pair_gain(a_q_norm), _pair_gain(b_k_norm), _pair_gain(b_q_norm),
             c_kv_lat_norm[:, None, :], c_q_lat_norm[:, None, :], _c_gain(c_k_norm), _c_gain(c_q_norm))
    tables = _rope_tables(TM_PROJ)
    ffn1_g, mix_g, ffn2_g = ffn1_norm[:, None, :], mix_norm[:, None, :], ffn2_norm[:, None, :]

    cond = jnp.concatenate([c, c_ctx[None, :], jnp.zeros((GROUP_PAD - N_GROUPS, D_MODEL), F32)], axis=0)
    mods = _modulation(cond, ada_w, ada_b).reshape(DEPTH * N_MOD * GROUP_PAD, 1, D_MODEL)

    xs = None
    for l in range(DEPTH):
        last = l == DEPTH - 1
        if l == 0:
            xs = _ffn(x.reshape(N_LAT, D_MODEL), mods, l, 0, ffn1_g, f1_in, f1_out, ctx.reshape(N_CTX, D_MODEL))
        else:
            xs = _ffn(xs, mods, l, 0, ffn1_g, f1_in, f1_out)
        kvab, qa, qb, qc, kc, vc = _project(xs, mods, l, mix_g, wp, wkc, wvc, wqc, gains, tables)
        ys = [_latent_window(qa, kvab, a_sink, l),
              _latent_dense_shared(qb, kvab, 2, 3, l, "attn_b_latent"),
              _latent_dense_split(qc, kc, vc)]
        if not last:
            ys = list(zip(ys, [_context_shared(qa, kvab, 0, 1, a_sink, l, "attn_a_context"),
                               _context_shared(qb, kvab, 2, 3, None, l, "attn_b_context"),
                               _context_split(qc, kc, vc)]))
        xs = _merge(xs, mods, l, mix_g, wg, ys, wa, wb, wc, wo, N_LAT if last else N_ROWS)
        xs = _ffn(xs, mods, l, 6, ffn2_g, f2_in, f2_out)
    return xs.reshape(BATCH, SEQ, D_MODEL)
```

```python
import functools
import math

import numpy as np
import jax
import jax.numpy as jnp
from jax import lax
from jax.experimental import pallas as pl
from jax.experimental.pallas import tpu as pltpu

F32 = jnp.float32
BF16 = jnp.bfloat16

D_MODEL = 1024
BATCH = 8
SEQ = 2048
DEPTH = 2
GRID_W = 64
CTX_LEN = 256
HEAD_DIM = 64
WINDOW = 128
ROPE_THETA = 10000.0
NORM_EPS = 1e-6
N_MOD = 9
D_FF = 2816
N_HEADS = 8
C_NOPE_DIM = 64
C_ROPE_DIM = 32
C_QK_DIM = C_NOPE_DIM + C_ROPE_DIM
C_V_DIM = 64
C_Q_RANK = 384
C_KV_RANK = 256
KV_COLS = 800
Q_COLS = 1408
GATE_COLS = 3 * D_MODEL

LANES = 128
N_LAT = BATCH * SEQ
N_CTX = BATCH * CTX_LEN
N_ROWS = N_LAT + N_CTX
N_GROUPS = BATCH + 1
GROUP_PAD = 16
CTX_BLK0 = N_LAT // CTX_LEN

VMEM_LIMIT = 56 * 1024 * 1024
LOG2E = math.log2(math.e)

PB_AK, PB_AV, PB_BK, PB_BV, PB_CKV, PB_AQ, PB_BQ, PB_CKR, PB_CQ, PB_END = 0, 1, 2, 3, 4, 6, 10, 14, 15, 18
PROJ_COLS = PB_END * LANES

TM_FFN = 512
TM_PROJ = 512
TM_MERGE = 1024
TQ_DENSE_SHARED = 1024
TQ_DENSE_SPLIT = 1024
ATTN_UNIT_ROWS = 512
TQ_WINDOW = 256
WINDOW_SUBTILES = 8

_SMEM_SPEC = pl.BlockSpec(memory_space=pltpu.SMEM)


def _proj_weight_natural(w_in):
    def zeros(n):
        return jnp.zeros(w_in.shape[:-1] + (n,), w_in.dtype)

    q0 = KV_COLS
    blocks = [w_in[..., 0:768]]
    for base in (q0, q0 + 512):
        for j in range(4):
            blocks += [w_in[..., base + j * 64:base + (j + 1) * 64],
                       w_in[..., base + (j + 4) * 64:base + (j + 5) * 64]]
    blocks += [zeros(C_NOPE_DIM), w_in[..., 768:800], zeros(32)]
    blocks.append(w_in[..., q0 + 1024:q0 + 1024 + C_Q_RANK])
    return jnp.concatenate(blocks, axis=-1).astype(BF16)


PROJ_BLOCK_KINDS = (("pair", "id") * 2 + ("id",) * 2 + ("pair",) * 8 + ("c",) + ("id",) * 3)


def _pad_heads(w, n_used):
    lead = w.shape[:-1]
    blk = jnp.concatenate([w[..., :n_used], jnp.zeros(lead + (LANES - n_used,), w.dtype)], axis=-1)
    return blk.reshape(lead[:-1] + (N_HEADS * LANES,)).astype(BF16)


def _perm_matrix(kind):
    r = lax.broadcasted_iota(jnp.int32, (LANES, LANES), 0)
    c = lax.broadcasted_iota(jnp.int32, (LANES, LANES), 1)
    if kind == "pair":
        src = ((c >> 5) & 1) * 64 + 2 * (c & 31) + (c >> 6)
    else:
        src = jnp.where(c < 16, 64 + 2 * c,
                        jnp.where(c < 64, c - 16,
                                  jnp.where(c < 80, 2 * c - 63,
                                            jnp.where(c < 96, c - 32, -1))))
    return jnp.where(r == src, 1.0, 0.0).astype(BF16)


def _permute_kernel(w_ref, o_ref, *, kinds):
    mats = {k: _perm_matrix(k) for k in set(kinds) if k != "id"}
    for j, kind in enumerate(kinds):
        sl = slice(j * LANES, (j + 1) * LANES)
        if kind == "id":
            o_ref[:, sl] = w_ref[:, sl]
        else:
            o_ref[:, sl] = _dot(w_ref[:, sl], mats[kind]).astype(BF16)


def _permute_cols(w, kinds):
    n_l, r, c = w.shape
    spec = pl.BlockSpec((None, r, c), lambda l: (l, 0, 0))
    return pl.pallas_call(
        functools.partial(_permute_kernel, kinds=kinds),
        grid=(n_l,),
        in_specs=[spec],
        out_specs=spec,
        out_shape=jax.ShapeDtypeStruct(w.shape, BF16),
        compiler_params=pltpu.CompilerParams(dimension_semantics=("parallel",), vmem_limit_bytes=VMEM_LIMIT),
        name="permute_cols",
    )(w)


def _pair_gain(g):
    e, o = g[:, 0::2], g[:, 1::2]
    return jnp.stack([jnp.concatenate([e, e, o, o], axis=-1), jnp.concatenate([o, o, e, e], axis=-1)], axis=1)


def _c_gain(g):
    nope, rope = g[:, :C_NOPE_DIM], g[:, C_NOPE_DIM:]
    zeros = jnp.zeros((g.shape[0], 32), g.dtype)
    lo = jnp.concatenate([rope[:, 0::2], nope[:, :48]], axis=-1)
    hi = jnp.concatenate([rope[:, 1::2], nope[:, 48:], zeros], axis=-1)
    return jnp.stack([jnp.concatenate([lo, hi], axis=-1), jnp.concatenate([hi, lo], axis=-1)], axis=1)


def _rope_tables(tm):
    pos = np.arange(SEQ)
    row = (pos // GRID_W).astype(np.float32)
    col = (pos % GRID_W).astype(np.float32)

    def angles(rot_dim):
        n_freq = rot_dim // 4
        inv = np.float64(ROPE_THETA) ** (-np.arange(n_freq, dtype=np.float64) / n_freq)
        return np.concatenate([row[:, None] * inv, col[:, None] * inv], axis=-1)

    a64 = angles(HEAD_DIM)
    c, s = np.cos(a64), np.sin(a64)
    cos_ab = np.concatenate([c, c, c, c], axis=-1)
    sin_ab = np.concatenate([-s, -s, s, s], axis=-1)
    a32 = angles(C_ROPE_DIM)
    c, s = np.cos(a32), np.sin(a32)
    one48, zero48 = np.ones((SEQ, 48)), np.zeros((SEQ, 48))
    cos_c = np.concatenate([c, one48, c, one48], axis=-1)
    sin_c = np.concatenate([-s, zero48, s, zero48], axis=-1)
    ident_c, ident_s = np.ones((tm, LANES)), np.zeros((tm, LANES))
    return tuple(jnp.asarray(np.concatenate(pair).astype(np.float32))
                 for pair in ((cos_ab, ident_c), (sin_ab, ident_s), (cos_c, ident_c), (sin_c, ident_s)))


def _adaln_bf16(x, gain, shift, scale):
    inv = lax.rsqrt(jnp.mean(x * x, axis=-1, keepdims=True) + NORM_EPS)
    return ((x * inv) * (gain * (1.0 + scale)) + shift).astype(BF16)


def _rms(x, gain, n):
    return x * lax.rsqrt(jnp.sum(x * x, axis=-1, keepdims=True) * (1.0 / n) + NORM_EPS) * gain


def _rope_gain_tables(g_ref, cos, sin, scale):
    return cos * (g_ref[0:1, :] * scale), sin * (g_ref[1:2, :] * scale)


def _rotate_bf16(y, gcos, gsin):
    return (y * gcos + pltpu.roll(y, 64, axis=1) * gsin).astype(BF16)


def _group_ones(width, group_of):
    r = lax.broadcasted_iota(jnp.int32, (width, width), 0)
    c = lax.broadcasted_iota(jnp.int32, (width, width), 1)
    return jnp.where(group_of(r) == group_of(c), 1.0, 0.0).astype(BF16)


def _inv_norm_mxu(x, ones_mat, n):
    return lax.rsqrt(_dot((x * x).astype(BF16), ones_mat) + n * NORM_EPS)


def _dot(a, b):
    return jnp.dot(a, b, preferred_element_type=F32)


def _dot_nt(a, b):
    return lax.dot_general(a, b, (((1,), (1,)), ((), ())), preferred_element_type=F32)


def _mod_kernel(cond_ref, w_ref, b_ref, o_ref):
    c = cond_ref[...]
    s = (c * jax.nn.sigmoid(c)).astype(BF16)
    o_ref[...] = _dot(s, w_ref[...].astype(BF16)) + b_ref[...]


def _modulation(cond, ada_w, ada_b):
    return pl.pallas_call(
        _mod_kernel,
        grid=(DEPTH, N_MOD),
        in_specs=[
            pl.BlockSpec((GROUP_PAD, D_MODEL), lambda l, k: (0, 0)),
            pl.BlockSpec((None, D_MODEL, D_MODEL), lambda l, k: (l, 0, k)),
            pl.BlockSpec((None, 1, D_MODEL), lambda l, k: (l, 0, k)),
        ],
        out_specs=pl.BlockSpec((None, None, GROUP_PAD, D_MODEL), lambda l, k: (l, k, 0, 0)),
        out_shape=jax.ShapeDtypeStruct((DEPTH, N_MOD, GROUP_PAD, D_MODEL), F32),
        compiler_params=pltpu.CompilerParams(
            dimension_semantics=("arbitrary", "arbitrary"), vmem_limit_bytes=VMEM_LIMIT),
        name="modulation",
    )(cond, ada_w, ada_b.reshape(DEPTH, 1, N_MOD * D_MODEL))


def _mod_spec(layer, k, tm):
    base = (layer * N_MOD + k) * GROUP_PAD
    return pl.BlockSpec((1, 1, D_MODEL), lambda i: (base + (i * tm) // SEQ, 0, 0))


def _split_row_specs(tm, cols):
    n_lat = N_LAT // tm
    return (pl.BlockSpec((tm, cols), lambda i: (jnp.minimum(i, n_lat - 1), 0)),
            pl.BlockSpec((tm, cols), lambda i: (jnp.maximum(i - n_lat, 0), 0)))


def _pick_rows(lat_ref, ctx_ref, tm):
    return jnp.where(pl.program_id(0) >= N_LAT // tm, ctx_ref[...], lat_ref[...])


def _ffn_kernel(*refs, split):
    if split:
        x = _pick_rows(refs[0], refs[1], TM_FFN)
        refs = refs[1:]
    else:
        x = refs[0][...]
    _, sh_ref, sc_ref, gt_ref, g_ref, wi_ref, wo_ref, o_ref = refs
    h = _adaln_bf16(x, g_ref[...], sh_ref[0], sc_ref[0])
    gate = _dot(h, wi_ref[:, :D_FF])
    up = _dot(h, wi_ref[:, D_FF:])
    act = (gate * jax.nn.sigmoid(gate) * up).astype(BF16)
    o_ref[...] = x + 0.5 * gt_ref[0] * _dot(act, wo_ref[...])


def _ffn(x, mods, layer, k0, gain, w_in, w_out, x_ctx=None):
    tm = TM_FFN
    split = x_ctx is not None
    n_rows = x.shape[0] + (x_ctx.shape[0] if split else 0)
    resident = pl.Buffered(1)
    x_specs = list(_split_row_specs(tm, D_MODEL)) if split else [pl.BlockSpec((tm, D_MODEL), lambda i: (i, 0))]
    return pl.pallas_call(
        functools.partial(_ffn_kernel, split=split),
        grid=(n_rows // tm,),
        in_specs=x_specs + [
            _mod_spec(layer, k0, tm), _mod_spec(layer, k0 + 1, tm), _mod_spec(layer, k0 + 2, tm),
            pl.BlockSpec((None, 1, D_MODEL), lambda i: (layer, 0, 0)),
            pl.BlockSpec((None, D_MODEL, 2 * D_FF), lambda i: (layer, 0, 0), pipeline_mode=resident),
            pl.BlockSpec((None, D_FF, D_MODEL), lambda i: (layer, 0, 0), pipeline_mode=resident),
        ],
        out_specs=pl.BlockSpec((tm, D_MODEL), lambda i: (i, 0)),
        out_shape=jax.ShapeDtypeStruct((n_rows, D_MODEL), F32),
        compiler_params=pltpu.CompilerParams(
            dimension_semantics=("parallel",), vmem_limit_bytes=VMEM_LIMIT),
        name="ffn",
    )(*([x, x_ctx] if split else [x]), mods, mods, mods, gain, w_in, w_out)


ST_KVAB, ST_QA, ST_QB, ST_KC, ST_QC, ST_VC, ST_END = 0, 4, 8, 12, 20, 28, 32


def _proj_kernel(x_ref, sh_ref, sc_ref, g_ref, wp_ref, wkc_ref, wvc_ref, wqc_ref,
                 gak_ref, gaq_ref, gbk_ref, gbq_ref, gkvl_ref, gql_ref, gck_ref, gcq_ref,
                 cab_ref, sab_ref, cc_ref, sc_c_ref,
                 kvab_ref, qa_ref, qb_ref, qc_ref, kc_ref, vc_ref, stage_ref):
    @pl.when(pl.program_id(0) == 0)
    def _():
        stage_ref[...] = jnp.zeros_like(stage_ref)

    def staged(b, n=1):
        return stage_ref[:, b * LANES:(b + n) * LANES]

    def stage(b, val):
        stage_ref[:, b * LANES:b * LANES + val.shape[1]] = val

    cab, sab, cc, sc_c = cab_ref[...], sab_ref[...], cc_ref[...], sc_c_ref[...]
    root_hd, root_c = HEAD_DIM ** 0.5, C_QK_DIM ** 0.5
    for b, g_ref_k in ((0, gak_ref), (2, gbk_ref)):
        gcos, gsin = _rope_gain_tables(g_ref_k, cab, sab, root_hd)
        kvab_ref[:, b * LANES:(b + 1) * LANES] = _rotate_bf16(staged(ST_KVAB + b), gcos, gsin)
        kvab_ref[:, (b + 1) * LANES:(b + 2) * LANES] = staged(ST_KVAB + b + 1).astype(BF16)
    for st, g_ref_q, q_ref in ((ST_QA, gaq_ref, qa_ref), (ST_QB, gbq_ref, qb_ref)):
        gcos, gsin = _rope_gain_tables(g_ref_q, cab, sab, root_hd * HEAD_DIM ** -0.5 * LOG2E)
        for j in range(4):
            q_ref[:, j * LANES:(j + 1) * LANES] = _rotate_bf16(staged(st + j), gcos, gsin)
    gcos_k, gsin_k = _rope_gain_tables(gck_ref, cc, sc_c, root_c)
    gcos_q, gsin_q = _rope_gain_tables(gcq_ref, cc, sc_c, root_c * C_QK_DIM ** -0.5 * LOG2E)
    for hd in range(N_HEADS):
        sl = slice(hd * LANES, (hd + 1) * LANES)
        kc_ref[:, sl] = _rotate_bf16(staged(ST_KC + hd), gcos_k, gsin_k)
        qc_ref[:, sl] = _rotate_bf16(staged(ST_QC + hd), gcos_q, gsin_q)
    vc_ref[...] = staged(ST_VC, 4).astype(BF16)

    h = _adaln_bf16(x_ref[...], g_ref[...], sh_ref[0], sc_ref[0])

    raw = _dot(h, wp_ref[...])

    def project(b, n):
        return raw[:, b * LANES:(b + n) * LANES]

    pair_ones = _group_ones(LANES, lambda l: (l >> 5) & 1)
    pair_ones2 = _group_ones(2 * LANES, lambda l: ((l >> 5) & 1) + 2 * (l >> 7))

    for b in (PB_AK, PB_BK):
        r = project(b, 2)
        k = r[:, :LANES]
        stage(ST_KVAB + b, k * _inv_norm_mxu(k, pair_ones, HEAD_DIM))
        stage(ST_KVAB + b + 1, r[:, LANES:])
    for st, pb in ((ST_QA, PB_AQ), (ST_QB, PB_BQ)):
        for j in (0, 2):
            r = project(pb + j, 2)
            stage(st + j, r * _inv_norm_mxu(r, pair_ones2, HEAD_DIM))

    ckv = _rms(project(PB_CKV, 2), gkvl_ref[...], C_KV_RANK).astype(BF16)
    stage(ST_VC, _dot(ckv, wvc_ref[...]))
    r = project(PB_CKR, 4)
    cq = _rms(r[:, LANES:], gql_ref[...], C_Q_RANK).astype(BF16)
    k_up = _dot(ckv, wkc_ref[...])
    q_up = _dot(cq, wqc_ref[...])
    k_rope = r[:, :LANES]

    def unit_norm(v):
        return v * lax.rsqrt(jnp.sum(v * v, axis=-1, keepdims=True) + C_QK_DIM * NORM_EPS)

    for hd in range(N_HEADS):
        sl = slice(hd * LANES, (hd + 1) * LANES)
        stage(ST_KC + hd, unit_norm(k_up[:, sl] + k_rope))
        stage(ST_QC + hd, unit_norm(q_up[:, sl]))


def _project(x, mods, layer, gain, wp, wkc, wvc, wqc, gains, tables):
    tm = TM_PROJ
    n_tiles = N_ROWS // tm
    n_lat_tiles = N_LAT // tm
    tiles_per_seq = SEQ // tm

    def cur(i):
        return jnp.minimum(i, n_tiles - 1)

    def prev(i):
        return jnp.maximum(i - 1, 0)

    def tab_idx(i):
        t = prev(i)
        return (jnp.where(t < n_lat_tiles, t % tiles_per_seq, tiles_per_seq), 0)

    def mod_spec(k):
        base = (layer * N_MOD + k) * GROUP_PAD
        return pl.BlockSpec((1, 1, D_MODEL), lambda i: (base + (cur(i) * tm) // SEQ, 0, 0))

    def lvec(n):
        return pl.BlockSpec((None, 1, n), lambda i: (layer, 0, 0))

    def lmat(r, c, **kw):
        return pl.BlockSpec((None, r, c), lambda i: (layer, 0, 0), **kw)

    def rows(c):
        return pl.BlockSpec((tm, c), lambda i: (prev(i), 0)), jax.ShapeDtypeStruct((N_ROWS, c), BF16)

    out = [rows(4 * LANES), rows(4 * LANES), rows(4 * LANES), rows(8 * LANES), rows(8 * LANES), rows(4 * LANES)]
    tab_spec = pl.BlockSpec((tm, LANES), tab_idx)
    resident = dict(pipeline_mode=pl.Buffered(1))
    return pl.pallas_call(
        _proj_kernel,
        grid=(n_tiles + 1,),
        in_specs=[
            pl.BlockSpec((tm, D_MODEL), lambda i: (cur(i), 0)),
            mod_spec(3), mod_spec(4),
            lvec(D_MODEL),
            lmat(D_MODEL, PROJ_COLS, **resident), lmat(C_KV_RANK, 8 * LANES, **resident),
            lmat(C_KV_RANK, 4 * LANES, **resident), lmat(C_Q_RANK, 8 * LANES, **resident),
            lmat(2, LANES), lmat(2, LANES), lmat(2, LANES), lmat(2, LANES), lvec(C_KV_RANK), lvec(C_Q_RANK),
            lmat(2, LANES), lmat(2, LANES),
            tab_spec, tab_spec, tab_spec, tab_spec,
        ],
        out_specs=[o[0] for o in out],
        out_shape=[o[1] for o in out],
        scratch_shapes=[pltpu.VMEM((tm, ST_END * LANES), F32)],
        compiler_params=pltpu.CompilerParams(
            dimension_semantics=("arbitrary",), vmem_limit_bytes=VMEM_LIMIT),
        name="project",
    )(x, mods, mods, gain, wp, wkc, wvc, wqc, *gains, *tables)


def _with_ones(v):
    return jnp.concatenate([v, jnp.ones_like(v)], axis=1)


def _attend(q, segs, sink_col):
    scores = []
    for k, _, bias in segs:
        s = _dot_nt(q, k)
        scores.append(s if bias is None else s + bias)
    m = functools.reduce(jnp.maximum, [jnp.max(s, axis=-1, keepdims=True) for s in scores])
    if sink_col is not None:
        m = jnp.maximum(m, sink_col)
    acc = None
    for s, (_, v, _) in zip(scores, segs):
        part = _dot(jnp.exp2(s - m).astype(BF16), v)
        acc = part if acc is None else acc + part
    denom = acc[:, LANES:]
    if sink_col is not None:
        denom = denom + jnp.exp2(sink_col - m)
    return acc[:, :LANES] / denom


def _pair_masks():
    lane = lax.broadcasted_iota(jnp.int32, (1, LANES), 1)
    return (lane & 63) < 32, lane < 64


def _dense_shared_kernel(*refs, layer, tq, n_seg, use_sink):
    q_ref = refs[0]
    kv = refs[1:1 + 2 * n_seg]
    sink_ref = refs[1 + 2 * n_seg] if use_sink else None
    o_ref = refs[-1]
    head0, low = _pair_masks()
    zero = jnp.zeros((), BF16)
    unit = min(tq, ATTN_UNIT_ROWS)
    segs = [(kv[2 * s][...], _with_ones(kv[2 * s + 1][...]), None) for s in range(n_seg)]
    for sub in range(tq // unit):
        rows = slice(sub * unit, (sub + 1) * unit)
        for j in range(4):
            qb = q_ref[rows, j * LANES:(j + 1) * LANES]
            outs = []
            for half in range(2):
                q = jnp.where(head0, qb, zero) if half == 0 else jnp.where(head0, zero, qb)
                sink_col = (jnp.full((unit, 1), sink_ref[layer, j + 4 * half] * LOG2E, F32) if use_sink else None)
                outs.append(_attend(q, segs, sink_col))
            o_ref[rows, j * LANES:(j + 1) * LANES] = jnp.where(low, outs[0], outs[1]).astype(o_ref.dtype)


def _dense_split_kernel(*refs, tq, n_seg):
    q_ref = refs[0]
    kv = refs[1:1 + 2 * n_seg]
    o_ref = refs[-1]
    _, low = _pair_masks()
    unit = min(tq, ATTN_UNIT_ROWS)
    for j in range(4):
        vsl = slice(j * LANES, (j + 1) * LANES)
        vals = [_with_ones(kv[2 * s + 1][:, vsl]) for s in range(n_seg)]
        for sub in range(tq // unit):
            rows = slice(sub * unit, (sub + 1) * unit)
            outs = []
            for half in range(2):
                ksl = slice((2 * j + half) * LANES, (2 * j + half + 1) * LANES)
                segs = [(kv[2 * s][:, ksl], vals[s], None) for s in range(n_seg)]
                outs.append(_attend(q_ref[rows, ksl], segs, None))
            o_ref[rows, vsl] = jnp.where(low, outs[0], outs[1]).astype(o_ref.dtype)


def _window_kernel(q_ref, kl_ref, vl_ref, kc_ref, vc_ref, sink_ref, o_ref, *, layer, tq, n_sub):
    span = tq + 2 * WINDOW
    head0, low = _pair_masks()
    zero = jnp.zeros((), BF16)
    ctx_seg = (kc_ref[...], _with_ones(vc_ref[...]), None)
    for sub in range(n_sub):
        rows = slice(sub * tq, (sub + 1) * tq)
        start = (pl.program_id(1) * n_sub + sub) * tq
        lo = pl.multiple_of(jnp.clip(start - WINDOW, 0, SEQ - span), WINDOW)
        q_pos = start + lax.broadcasted_iota(jnp.int32, (tq, span), 0)
        k_pos = lo + lax.broadcasted_iota(jnp.int32, (tq, span), 1)
        band = jnp.where(jnp.abs(q_pos - k_pos) <= WINDOW, 0.0, -jnp.inf)
        band = jnp.concatenate([band, band], axis=0)
        segs = [(kl_ref[pl.ds(lo, span), :], _with_ones(vl_ref[pl.ds(lo, span), :]), band), ctx_seg]
        for j in range(4):
            qb = q_ref[rows, j * LANES:(j + 1) * LANES]
            q = jnp.concatenate([jnp.where(head0, qb, zero), jnp.where(head0, zero, qb)], axis=0)
            sink_col = jnp.concatenate([jnp.full((tq, 1), sink_ref[layer, j] * LOG2E, F32),
                                        jnp.full((tq, 1), sink_ref[layer, j + 4] * LOG2E, F32)], axis=0)
            o = _attend(q, segs, sink_col)
            o_ref[rows, j * LANES:(j + 1) * LANES] = jnp.where(low, o[:tq], o[tq:]).astype(o_ref.dtype)


def _attn_params(n_grid):
    return pltpu.CompilerParams(dimension_semantics=("parallel",) * n_grid, vmem_limit_bytes=VMEM_LIMIT)


def _latent_dense_shared(q, kvab, k_blk, v_blk, layer, name):
    tq = TQ_DENSE_SHARED
    nq = SEQ // tq
    kern = functools.partial(_dense_shared_kernel, layer=layer, tq=tq, n_seg=2, use_sink=False)
    return pl.pallas_call(
        kern,
        grid=(BATCH, nq),
        in_specs=[
            pl.BlockSpec((tq, 4 * LANES), lambda b, i: (b * nq + i, 0)),
            pl.BlockSpec((SEQ, LANES), lambda b, i: (b, k_blk)),
            pl.BlockSpec((SEQ, LANES), lambda b, i: (b, v_blk)),
            pl.BlockSpec((CTX_LEN, LANES), lambda b, i: (CTX_BLK0 + b, k_blk)),
            pl.BlockSpec((CTX_LEN, LANES), lambda b, i: (CTX_BLK0 + b, v_blk)),
        ],
        out_specs=pl.BlockSpec((tq, 4 * LANES), lambda b, i: (b * nq + i, 0)),
        out_shape=jax.ShapeDtypeStruct((N_LAT, 4 * LANES), BF16),
        compiler_params=_attn_params(2),
        name=name,
    )(q, kvab, kvab, kvab, kvab)


def _latent_window(q, kvab, sink, layer):
    n_sub = WINDOW_SUBTILES
    tq = TQ_WINDOW * n_sub
    nq = SEQ // tq
    kern = functools.partial(_window_kernel, layer=layer, tq=TQ_WINDOW, n_sub=n_sub)
    return pl.pallas_call(
        kern,
        grid=(BATCH, nq),
        in_specs=[
            pl.BlockSpec((tq, 4 * LANES), lambda b, i: (b * nq + i, 0)),
            pl.BlockSpec((SEQ, LANES), lambda b, i: (b, 0)),
            pl.BlockSpec((SEQ, LANES), lambda b, i: (b, 1)),
            pl.BlockSpec((CTX_LEN, LANES), lambda b, i: (CTX_BLK0 + b, 0)),
            pl.BlockSpec((CTX_LEN, LANES), lambda b, i: (CTX_BLK0 + b, 1)),
            _SMEM_SPEC,
        ],
        out_specs=pl.BlockSpec((tq, 4 * LANES), lambda b, i: (b * nq + i, 0)),
        out_shape=jax.ShapeDtypeStruct((N_LAT, 4 * LANES), BF16),
        compiler_params=_attn_params(2),
        name="attn_a_window",
    )(q, kvab, kvab, kvab, kvab, sink)


def _latent_dense_split(q, k, v):
    tq = TQ_DENSE_SPLIT
    nq = SEQ // tq
    kern = functools.partial(_dense_split_kernel, tq=tq, n_seg=2)
    return pl.pallas_call(
        kern,
        grid=(BATCH, nq),
        in_specs=[
            pl.BlockSpec((tq, 8 * LANES), lambda b, i: (b * nq + i, 0)),
            pl.BlockSpec((SEQ, 8 * LANES), lambda b, i: (b, 0)),
            pl.BlockSpec((SEQ, 4 * LANES), lambda b, i: (b, 0)),
            pl.BlockSpec((CTX_LEN, 8 * LANES), lambda b, i: (CTX_BLK0 + b, 0)),
            pl.BlockSpec((CTX_LEN, 4 * LANES), lambda b, i: (CTX_BLK0 + b, 0)),
        ],
        out_specs=pl.BlockSpec((tq, 4 * LANES), lambda b, i: (b * nq + i, 0)),
        out_shape=jax.ShapeDtypeStruct((N_LAT, 4 * LANES), BF16),
        compiler_params=_attn_params(2),
        name="attn_c_latent",
    )(q, k, v, k, v)


def _context_shared(q, kvab, k_blk, v_blk, sink, layer, name):
    tq = CTX_LEN
    use_sink = sink is not None
    kern = functools.partial(_dense_shared_kernel, layer=layer, tq=tq, n_seg=1, use_sink=use_sink)
    in_specs = [
        pl.BlockSpec((tq, 4 * LANES), lambda b: (CTX_BLK0 + b, 0)),
        pl.BlockSpec((CTX_LEN, LANES), lambda b: (CTX_BLK0 + b, k_blk)),
        pl.BlockSpec((CTX_LEN, LANES), lambda b: (CTX_BLK0 + b, v_blk)),
    ]
    args = [q, kvab, kvab]
    if use_sink:
        in_specs.append(_SMEM_SPEC)
        args.append(sink)
    return pl.pallas_call(
        kern,
        grid=(BATCH,),
        in_specs=in_specs,
        out_specs=pl.BlockSpec((tq, 4 * LANES), lambda b: (b, 0)),
        out_shape=jax.ShapeDtypeStruct((N_CTX, 4 * LANES), BF16),
        compiler_params=_attn_params(1),
        name=name,
    )(*args)


def _context_split(q, k, v):
    tq = CTX_LEN
    kern = functools.partial(_dense_split_kernel, tq=tq, n_seg=1)
    return pl.pallas_call(
        kern,
        grid=(BATCH,),
        in_specs=[
            pl.BlockSpec((tq, 8 * LANES), lambda b: (CTX_BLK0 + b, 0)),
            pl.BlockSpec((CTX_LEN, 8 * LANES), lambda b: (CTX_BLK0 + b, 0)),
            pl.BlockSpec((CTX_LEN, 4 * LANES), lambda b: (CTX_BLK0 + b, 0)),
        ],
        out_specs=pl.BlockSpec((tq, 4 * LANES), lambda b: (b, 0)),
        out_shape=jax.ShapeDtypeStruct((N_CTX, 4 * LANES), BF16),
        compiler_params=_attn_params(1),
        name="attn_c_context",
    )(q, k, v)


def _merge_kernel(x_ref, sh_ref, sc_ref, gt_ref, g_ref, wg_ref, *refs, split):
    n_y = 6 if split else 3
    y_refs, (wa_ref, wb_ref, wc_ref, wo_ref, o_ref) = refs[:n_y], refs[n_y:]
    x = x_ref[...]
    u = _adaln_bf16(x, g_ref[...], sh_ref[0], sc_ref[0])
    m = None
    for idx, w_ref in enumerate((wa_ref, wb_ref, wc_ref)):
        y = _pick_rows(y_refs[2 * idx], y_refs[2 * idx + 1], TM_MERGE) if split else y_refs[idx][...]
        gate = _dot(u, wg_ref[:, idx * D_MODEL:(idx + 1) * D_MODEL])
        term = jax.nn.sigmoid(gate) * _dot(y, w_ref[...])
        m = term if m is None else m + term
    o_ref[...] = x + gt_ref[0] * _dot(m.astype(BF16), wo_ref[...])


def _merge(x, mods, layer, gain, wg, ys, wa, wb, wc, wo, n_rows):
    tm = TM_MERGE
    split = isinstance(ys[0], tuple)

    def lmat(r, c):
        return pl.BlockSpec((None, r, c), lambda i: (layer, 0, 0))

    if split:
        y_specs = list(_split_row_specs(tm, 4 * LANES)) * 3
        y_args = [a for pair in ys for a in pair]
    else:
        y_specs = [pl.BlockSpec((tm, 4 * LANES), lambda i: (i, 0))] * 3
        y_args = list(ys)
    return pl.pallas_call(
        functools.partial(_merge_kernel, split=split),
        grid=(n_rows // tm,),
        in_specs=[
            pl.BlockSpec((tm, D_MODEL), lambda i: (i, 0)),
            _mod_spec(layer, 3, tm), _mod_spec(layer, 4, tm), _mod_spec(layer, 5, tm),
            pl.BlockSpec((None, 1, D_MODEL), lambda i: (layer, 0, 0)),
            lmat(D_MODEL, GATE_COLS),
            *y_specs,
            lmat(4 * LANES, D_MODEL), lmat(4 * LANES, D_MODEL), lmat(4 * LANES, D_MODEL),
            lmat(D_MODEL, D_MODEL),
        ],
        out_specs=pl.BlockSpec((tm, D_MODEL), lambda i: (i, 0)),
        out_shape=jax.ShapeDtypeStruct((n_rows, D_MODEL), F32),
        compiler_params=pltpu.CompilerParams(
            dimension_semantics=("parallel",), vmem_limit_bytes=VMEM_LIMIT),
        name="merge",
    )(x, mods, mods, mods, gain, wg, *y_args, wa, wb, wc, wo)


def kernel(x, c, ctx, c_ctx, ada_w, ada_b, ffn1_norm, ffn1_w_in, ffn1_w_out, mix_norm, mix_w_in, a_q_norm, a_k_norm, a_sink, b_q_norm, b_k_norm, c_q_lat_norm, c_w_uq, c_kv_lat_norm, c_w_ukv, c_q_norm, c_k_norm, w_branch_a, w_branch_b, w_branch_c, mix_w_out, ffn2_norm, ffn2_w_in, ffn2_w_out):
    n_l = mix_w_in.shape[0]
    wp = _permute_cols(_proj_weight_natural(mix_w_in), PROJ_BLOCK_KINDS)
    wg = mix_w_in[:, :, KV_COLS + Q_COLS:].astype(BF16)
    ukv = c_w_ukv.reshape(n_l, C_KV_RANK, N_HEADS, C_NOPE_DIM + C_V_DIM)
    wkc = _permute_cols(_pad_heads(ukv, C_NOPE_DIM), ("c",) * N_HEADS)
    wvc = ukv[..., C_NOPE_DIM:].reshape(n_l, C_KV_RANK, N_HEADS * C_V_DIM).astype(BF16)
    uq = c_w_uq.reshape(n_l, C_Q_RANK, N_HEADS, C_QK_DIM)
    wqc = _permute_cols(_pad_heads(uq, C_QK_DIM), ("c",) * N_HEADS)

    def pair_rows(w):
        return w.reshape(n_l, 2, 4, HEAD_DIM, D_MODEL).transpose(0, 2, 1, 3, 4).reshape(n_l, 8 * HEAD_DIM, D_MODEL)

    wa = pair_rows(w_branch_a).astype(BF16)
    wb = pair_rows(w_branch_b).astype(BF16)
    wc = w_branch_c.astype(BF16)
    wo = mix_w_out.astype(BF16)
    f1_in, f1_out = ffn1_w_in.astype(BF16), ffn1_w_out.astype(BF16)
    f2_in, f2_out = ffn2_w_in.astype(BF16), ffn2_w_out.astype(BF16)
    gains = (_pair_gain(a_k_norm), _pair_gain(a_q_norm), _pair_gain(b_k_norm), _pair_gain(b_q_norm),
             c_kv_lat_norm[:, None, :], c_q_lat_norm[:, None, :], _c_gain(c_k_norm), _c_gain(c_q_norm))
    tables = _rope_tables(TM_PROJ)
    ffn1_g, mix_g, ffn2_g = ffn1_norm[:, None, :], mix_norm[:, None, :], ffn2_norm[:, None, :]

    cond = jnp.concatenate([c, c_ctx[None, :], jnp.zeros((GROUP_PAD - N_GROUPS, D_MODEL), F32)], axis=0)
    mods = _modulation(cond, ada_w, ada_b).reshape(DEPTH * N_MOD * GROUP_PAD, 1, D_MODEL)

    xs = None
    for l in range(DEPTH):
        last = l == DEPTH - 1
        if l == 0:
            xs = _ffn(x.reshape(N_LAT, D_MODEL), mods, l, 0, ffn1_g, f1_in, f1_out, ctx.reshape(N_CTX, D_MODEL))
        else:
            xs = _ffn(xs, mods, l, 0, ffn1_g, f1_in, f1_out)
        kvab, qa, qb, qc, kc, vc = _project(xs, mods, l, mix_g, wp, wkc, wvc, wqc, gains, tables)
        ys = [_latent_window(qa, kvab, a_sink, l),
              _latent_dense_shared(qb, kvab, 2, 3, l, "attn_b_latent"),
              _latent_dense_split(qc, kc, vc)]
        if not last:
            ys = list(zip(ys, [_context_shared(qa, kvab, 0, 1, a_sink, l, "attn_a_context"),
                               _context_shared(qb, kvab, 2, 3, None, l, "attn_b_context"),
                               _context_split(qc, kc, vc)]))
        xs = _merge(xs, mods, l, mix_g, wg, ys, wa, wb, wc, wo, N_LAT if last else N_ROWS)
        xs = _ffn(xs, mods, l, 6, ffn2_g, f2_in, f2_out)
    return xs.reshape(BATCH, SEQ, D_MODEL)
```
